```python
import jax
import jax.numpy as jnp
from jax import lax
import numpy as np

D_MODEL = 1024
BATCH = 8
SEQ = 4096
DEPTH = 2
DEC_BATCH = 32
DEC_SEQ = 8
PAST_LEN = 16384
PAGE_SIZE = 128

SB_HEADS = 8
SB_HEAD_DIM = 64
SB_WIDTH = SB_HEADS * SB_HEAD_DIM
SB_BLOCK = 128
SB_BIAS_INIT = -6.0
GDN_HEADS = 4
GDN_DK = 128
GDN_DV = 128
GDN_QK_WIDTH = GDN_HEADS * GDN_DK
GDN_V_WIDTH = GDN_HEADS * GDN_DV
GDN_QKV_WIDTH = 2 * GDN_QK_WIDTH + GDN_V_WIDTH
GDN_CONV = 4
GDN_CHUNK = 64
SC_WIDTH = D_MODEL
SC_CONV = 3
AB_IN_COLS = 4 * SB_WIDTH + GDN_QKV_WIDTH + GDN_V_WIDTH + 2 * GDN_HEADS
AB_MIX_WIDTH = SB_WIDTH + GDN_V_WIDTH
L_AB = (DEPTH + 1) // 2
L_C = DEPTH // 2
NORM_EPS = 1e-6

kernel_name = "stickbreak_gdn_shortconv_hybrid_step"


def rms_norm(x, w):
    xf = x.astype(jnp.float32)
    y = xf * lax.rsqrt(jnp.mean(xf * xf, axis=-1, keepdims=True) + NORM_EPS)
    return (y * w.astype(jnp.float32)).astype(x.dtype)


def l2norm(x):
    xf = x.astype(jnp.float32)
    return xf * lax.rsqrt(jnp.sum(xf * xf, axis=-1, keepdims=True) + NORM_EPS)


def causal_conv(x, buf, w):
    t = x.shape[1]
    width = w.shape[0]
    xx = jnp.concatenate([buf.astype(x.dtype), x], axis=1)
    y = xx[:, 0:t] * w[0]
    for i in range(1, width):
        y = y + xx[:, i:i + t] * w[i]
    return y, xx[:, t:]


def stick_breaking_attention(q, k, v, bias, q_offset):
    n, t, h, d = q.shape
    s_len = k.shape[1]
    qb = min(SB_BLOCK, t)
    nb = t // qb
    scale = d ** -0.5
    q_blocks = jnp.moveaxis(q.astype(jnp.float32).reshape(n, nb, qb, h, d), 1, 0)
    kf = k.astype(jnp.float32)
    vf = v.astype(jnp.float32)
    bf = bias.astype(jnp.float32)[None, :, None, None]
    key_pos = jnp.arange(s_len)

    def block(args):
        q_blk, start = args
        z = jnp.einsum('nqhd,nshd->nhqs', q_blk, kf) * scale + bf
        q_pos = q_offset + start + jnp.arange(qb)
        mask = key_pos[None, :] < q_pos[:, None]
        log_1m = jnp.where(mask, jax.nn.log_sigmoid(-z), 0.0)
        log_rest = lax.cumsum(log_1m, axis=3, reverse=True) - log_1m
        a = jnp.where(mask, jnp.exp(jax.nn.log_sigmoid(z) + log_rest), 0.0)
        return jnp.einsum('nhqs,nshd->nqhd', a, vf)

    out = lax.map(block, (q_blocks, jnp.arange(nb) * qb))
    return jnp.moveaxis(out, 0, 1).reshape(n, t, h, d).astype(q.dtype)


def gated_delta_rule(q, k, v, g, beta, state):
    f32 = jnp.float32
    n, t, h, dk = q.shape
    dv = v.shape[-1]
    c = min(GDN_CHUNK, t)
    pad = (-t) % c
    q, k, v, g, beta = [jnp.pad(a.astype(f32), [(0, 0), (0, pad)] + [(0, 0)] * (a.ndim - 2))
                        for a in (q, k, v, g, beta)]
    nc = (t + pad) // c

    def chunked(a):
        return jnp.moveaxis(a.reshape((n, nc, c, h) + a.shape[3:]), 3, 1)

    q, k, v, g, beta = [chunked(a) for a in (q, k, v, g, beta)]
    g_cum = jnp.cumsum(g, axis=-1)
    idx = jnp.arange(c)
    causal = idx[:, None] >= idx[None, :]
    strict = idx[:, None] > idx[None, :]
    decay = jnp.exp(jnp.where(causal, g_cum[..., :, None] - g_cum[..., None, :], -jnp.inf))
    k_beta = k * beta[..., None]
    m = jnp.where(strict, jnp.einsum('nhcid,nhcjd->nhcij', k_beta, k) * decay, 0.0)
    eye = jnp.eye(c, dtype=f32)
    t_inv = lax.linalg.triangular_solve(m + eye, jnp.broadcast_to(eye, m.shape),
                                        left_side=True, lower=True, unit_diagonal=True)
    u = t_inv @ (v * beta[..., None])
    w = t_inv @ (k_beta * jnp.exp(g_cum)[..., None])
    qk = jnp.where(causal, jnp.einsum('nhcid,nhcjd->nhcij', q, k) * decay, 0.0)

    def step(s, xs):
        q_c, k_c, u_c, w_c, qk_c, g_c = xs
        v_new = u_c - w_c @ s
        o_c = (q_c * jnp.exp(g_c)[..., None]) @ s + qk_c @ v_new
        g_last = g_c[..., -1:]
        s = s * jnp.exp(g_last)[..., None] + jnp.einsum(
            'nhck,nhcv->nhkv', k_c * jnp.exp(g_last - g_c)[..., None], v_new)
        return s, o_c

    xs = tuple(jnp.moveaxis(a, 2, 0) for a in (q, k, u, w, qk, g_cum))
    state, o = lax.scan(step, state.astype(f32), xs)
    o = jnp.moveaxis(jnp.moveaxis(o, 0, 2), 1, 3).reshape(n, nc * c, h, dv)[:, :t]
    return o, state


def ab_mixer(xn, w_in, sb_bias, conv_w, a_log, dt_bias, out_norm_w, w_out,
             past_k, past_v, q_offset, gdn_state, gdn_buf):
    n, t, _ = xn.shape
    sizes = [SB_WIDTH] * 4 + [GDN_QKV_WIDTH, GDN_V_WIDTH, GDN_HEADS, GDN_HEADS]
    splits = [int(s) for s in np.cumsum(sizes)[:-1]]
    proj = jnp.einsum('ntd,dc->ntc', xn, w_in)
    sb_q, sb_k, sb_v, sb_z, g_qkv, g_z, g_b, g_a = jnp.split(proj, splits, axis=-1)
    q = sb_q.reshape(n, t, SB_HEADS, SB_HEAD_DIM)
    k = sb_k.reshape(n, t, SB_HEADS, SB_HEAD_DIM)
    v = sb_v.reshape(n, t, SB_HEADS, SB_HEAD_DIM)
    if past_k is None:
        keys, vals = k, v
    else:
        keys = jnp.concatenate([past_k.astype(k.dtype), k], axis=1)
        vals = jnp.concatenate([past_v.astype(v.dtype), v], axis=1)
    o_a = stick_breaking_attention(q, keys, vals, sb_bias, q_offset).reshape(n, t, SB_WIDTH) * jax.nn.silu(sb_z)
    qkv, new_buf = causal_conv(g_qkv, gdn_buf, conv_w)
    qkv = jax.nn.silu(qkv)
    gq, gk, gv = jnp.split(qkv, [GDN_QK_WIDTH, 2 * GDN_QK_WIDTH], axis=-1)
    gq = l2norm(gq.reshape(n, t, GDN_HEADS, GDN_DK)) * (GDN_DK ** -0.5)
    gk = l2norm(gk.reshape(n, t, GDN_HEADS, GDN_DK))
    gv = gv.reshape(n, t, GDN_HEADS, GDN_DV)
    beta = jax.nn.sigmoid(g_b.astype(jnp.float32))
    g = -jnp.exp(a_log.astype(jnp.float32)) * jax.nn.softplus(
        g_a.astype(jnp.float32) + dt_bias.astype(jnp.float32))
    o_b, new_state = gated_delta_rule(gq, gk, gv, g, beta, gdn_state)
    o_b = rms_norm(o_b, out_norm_w).astype(xn.dtype).reshape(n, t, GDN_V_WIDTH) * jax.nn.silu(g_z)
    out = jnp.einsum('ntc,cd->ntd', jnp.concatenate([o_a, o_b], axis=-1), w_out)
    return out, k, v, new_state, new_buf


def sconv_mixer(xn, w_in, conv_w, w_out, conv_buf):
    proj = jnp.einsum('ntd,dc->ntc', xn, w_in)
    u, gate_b, gate_c, z = jnp.split(proj, 4, axis=-1)
    y, new_buf = causal_conv(gate_c * u, conv_buf, conv_w)
    out = jnp.einsum('ntc,cd->ntd', gate_b * y * jax.nn.silu(z), w_out)
    return out, new_buf


def gather_pages(pool, page_table):
    rows = pool[page_table]
    return rows.reshape(page_table.shape[0], page_table.shape[1] * pool.shape[1], pool.shape[2], pool.shape[3])


def run_trunk(x, pool_k, pool_v, page_table, gdn_state, gdn_buf, sc_buf,
              norm_w, ab_w_in, ab_sb_bias, ab_conv_w, ab_a_log, ab_dt_bias, ab_out_norm_w, ab_w_out,
              sc_w_in, sc_conv_w, sc_w_out, final_norm_w):
    h = x
    q_offset = 0 if page_table is None else page_table.shape[1] * pool_k.shape[2]
    new_k, new_v, new_s, new_gbuf, new_scbuf = [], [], [], [], []
    for layer in range(DEPTH):
        xn = rms_norm(h, norm_w[layer])
        i = layer // 2
        if layer % 2 == 0:
            if page_table is None:
                pk = pv = None
            else:
                pk = gather_pages(pool_k[i], page_table)
                pv = gather_pages(pool_v[i], page_table)
            out, k_rows, v_rows, s_i, buf_i = ab_mixer(
                xn, ab_w_in[i], ab_sb_bias[i], ab_conv_w[i], ab_a_log[i], ab_dt_bias[i], ab_out_norm_w[i],
                ab_w_out[i], pk, pv, q_offset, gdn_state[i], gdn_buf[i])
            new_k.append(k_rows)
            new_v.append(v_rows)
            new_s.append(s_i.astype(gdn_state.dtype))
            new_gbuf.append(buf_i)
        else:
            out, buf_i = sconv_mixer(xn, sc_w_in[i], sc_conv_w[i], sc_w_out[i], sc_buf[i])
            new_scbuf.append(buf_i)
        h = h + out
    y = rms_norm(h, final_norm_w)
    return y, jnp.stack(new_k), jnp.stack(new_v), jnp.stack(new_s), jnp.stack(new_gbuf), jnp.stack(new_scbuf)


def setup_inputs(seed: int = 0) -> dict:
    key = jax.random.key(seed)
    ks = jax.random.split(key, 24)
    f32 = jnp.float32
    n_pages = PAST_LEN // PAGE_SIZE
    n_used = DEC_BATCH * n_pages
    n_pool = n_used + max(1, n_used // 4)
    page_table = jax.random.permutation(ks[0], n_pool)[:n_used].reshape(DEC_BATCH, n_pages).astype(jnp.int32)
    dt = jnp.exp(jax.random.uniform(ks[1], (L_AB, GDN_HEADS), f32, np.log(1e-3), np.log(1e-1)))
    return {
        "x_prompt": jax.random.normal(ks[2], (BATCH, SEQ, D_MODEL), f32),
        "x_sample": jax.random.normal(ks[3], (DEC_BATCH, DEC_SEQ, D_MODEL), f32),
        "cache_sb_k": jax.random.normal(ks[4], (L_AB, n_pool, PAGE_SIZE, SB_HEADS, SB_HEAD_DIM), f32),
        "cache_sb_v": jax.random.normal(ks[5], (L_AB, n_pool, PAGE_SIZE, SB_HEADS, SB_HEAD_DIM), f32),
        "state_gdn": jax.random.normal(ks[6], (L_AB, DEC_BATCH, GDN_HEADS, GDN_DK, GDN_DV), f32) * GDN_DK ** -0.5,
        "state_gdn_conv": jax.random.normal(ks[7], (L_AB, DEC_BATCH, GDN_CONV - 1, GDN_QKV_WIDTH), f32),
        "state_sconv": jax.random.normal(ks[8], (L_C, DEC_BATCH, SC_CONV - 1, SC_WIDTH), f32),
        "page_table": page_table,
        "norm_w": 1.0 + 0.02 * jax.random.normal(ks[9], (DEPTH, D_MODEL), f32),
        "ab_w_in": jax.random.normal(ks[10], (L_AB, D_MODEL, AB_IN_COLS), f32) * D_MODEL ** -0.5,
        "ab_sb_bias": SB_BIAS_INIT + 0.5 * jax.random.normal(ks[19], (L_AB, SB_HEADS), f32),
        "ab_conv_w": jax.random.normal(ks[11], (L_AB, GDN_CONV, GDN_QKV_WIDTH), f32) * GDN_CONV ** -0.5,
        "ab_a_log": jnp.log(jax.random.uniform(ks[12], (L_AB, GDN_HEADS), f32, 1.0, 16.0)),
        "ab_dt_bias": dt + jnp.log(-jnp.expm1(-dt)),
        "ab_out_norm_w": 1.0 + 0.02 * jax.random.normal(ks[13], (L_AB, GDN_DV), f32),
        "ab_w_out": jax.random.normal(ks[14], (L_AB, AB_MIX_WIDTH, D_MODEL), f32) * AB_MIX_WIDTH ** -0.5,
        "sc_w_in": jax.random.normal(ks[15], (L_C, D_MODEL, 4 * SC_WIDTH), f32) * D_MODEL ** -0.5,
        "sc_conv_w": jax.random.normal(ks[16], (L_C, SC_CONV, SC_WIDTH), f32) * SC_CONV ** -0.5,
        "sc_w_out": jax.random.normal(ks[17], (L_C, SC_WIDTH, D_MODEL), f32) * SC_WIDTH ** -0.5,
        "final_norm_w": 1.0 + 0.02 * jax.random.normal(ks[18], (D_MODEL,), f32),
    }


def reference(x_prompt, x_sample, cache_sb_k, cache_sb_v, state_gdn, state_gdn_conv, state_sconv, page_table,
              norm_w, ab_w_in, ab_sb_bias, ab_conv_w, ab_a_log, ab_dt_bias, ab_out_norm_w, ab_w_out,
              sc_w_in, sc_conv_w, sc_w_out, final_norm_w):
    weights = (norm_w, ab_w_in, ab_sb_bias, ab_conv_w, ab_a_log, ab_dt_bias, ab_out_norm_w, ab_w_out,
               sc_w_in, sc_conv_w, sc_w_out, final_norm_w)
    nb = x_prompt.shape[0]
    dt = x_prompt.dtype
    p_gdn = jnp.zeros((L_AB, nb, GDN_HEADS, GDN_DK, GDN_DV), dt)
    p_gbuf = jnp.zeros((L_AB, nb, GDN_CONV - 1, GDN_QKV_WIDTH), dt)
    p_scbuf = jnp.zeros((L_C, nb, SC_CONV - 1, SC_WIDTH), dt)
    y_prompt, k_p, v_p, s_p, gb_p, sc_p = run_trunk(
        x_prompt, None, None, None, p_gdn, p_gbuf, p_scbuf, *weights)
    y_sample, k_s, v_s, s_s, gb_s, sc_s = run_trunk(
        x_sample, cache_sb_k, cache_sb_v, page_table, state_gdn, state_gdn_conv, state_sconv, *weights)
    return (y_prompt, y_sample, k_p, v_p, k_s, v_s, s_p, s_s, gb_p, gb_s, sc_p, sc_s)
```

```python
import functools
import math

import jax
import jax.numpy as jnp
from jax import lax
from jax.experimental import pallas as pl
from jax.experimental.pallas import tpu as pltpu

F32 = jnp.float32
BF16 = jnp.bfloat16
LOG2E = 1.4426950408889634
NORM_EPS = 1e-6
LANES = 128
SUBLANES = 8
VMEM_LIMIT_BYTES = 56 * 1024 * 1024

SB_HEAD_DIM = 64
GDN_HEADS = 4
GDN_DK = 128
GDN_CONV = 4
SC_CONV = 3


def _cparams(sem):
    return pltpu.CompilerParams(dimension_semantics=sem, vmem_limit_bytes=VMEM_LIMIT_BYTES)


def _silu(x):
    return x * (1.0 / (1.0 + jnp.exp(-x)))


def _rms(x, w):
    return x * lax.rsqrt(jnp.mean(x * x, axis=-1, keepdims=True) + NORM_EPS) * w


def _mm(a, b):
    return jnp.dot(a.astype(BF16), b.astype(BF16), preferred_element_type=F32)


def _mm_nt(a, b):
    return lax.dot_general(a.astype(BF16), b.astype(BF16), (((1,), (1,)), ((), ())),
                           preferred_element_type=F32)


def _split3(x):
    hi = x.astype(BF16)
    r = x - hi.astype(F32)
    mid = r.astype(BF16)
    lo = (r - mid.astype(F32)).astype(BF16)
    return hi, mid, lo


def _mm3(a, b):
    ah, am, _ = _split3(a)
    bh, bm, _ = _split3(b)
    d = functools.partial(jnp.dot, preferred_element_type=F32)
    return d(ah, bh) + (d(ah, bm) + d(am, bh))


def _mm_exact_lhs(a01, b):
    a = a01.astype(BF16)
    bh, bm, bl = _split3(b)
    d = functools.partial(jnp.dot, preferred_element_type=F32)
    return d(a, bh) + (d(a, bm) + d(a, bl))


def _softplus2(z):
    return jnp.maximum(z, 0.0) + jnp.log2(1.0 + jnp.exp2(-jnp.abs(z)))


def _proj0_kernel(x_ref, nw_ref, w_ref, wkvt_ref, wt_ref,
                  q_ref, kb_ref, vb_ref, kf_ref, vf_ref, sg_ref, gqkv_ref, gg_ref, tail_ref,
                  *, sbw, gqkvw, gvw, qscale, kv_transposed):
    x = x_ref[...]
    xn = _rms(x, nw_ref[...]).astype(BF16)

    def proj(lo, width):
        return jnp.dot(xn, w_ref[:, lo:lo + width], preferred_element_type=F32)

    q_ref[...] = (proj(0, sbw) * qscale).astype(BF16)
    if kv_transposed:
        nt = lambda wt: lax.dot_general(wt, xn, (((1,), (1,)), ((), ())), preferred_element_type=F32)
        k = nt(wkvt_ref[:sbw, :])
        v = nt(wkvt_ref[sbw:, :])
    else:
        k = proj(sbw, sbw)
        v = proj(2 * sbw, sbw)
    kf_ref[...] = k
    kb_ref[...] = k.astype(BF16)
    vf_ref[...] = v
    vb_ref[...] = v.astype(BF16)
    sg_ref[...] = _silu(proj(3 * sbw, sbw)).astype(BF16)
    gqkv_ref[...] = proj(4 * sbw, gqkvw)
    gg_ref[...] = _silu(proj(4 * sbw + gqkvw, gvw)).astype(BF16)
    tail_ref[...] = jnp.dot(xn, wt_ref[...], preferred_element_type=F32)


def _proj0(x, norm_w, w_main, w_kvt, w_tail, *, sbw, gqkvw, gvw, tm, kv_transposed):
    g, rows, d = x.shape
    assert rows % tm == 0
    qscale = LOG2E * SB_HEAD_DIM ** -0.5
    kern = functools.partial(_proj0_kernel, sbw=sbw, gqkvw=gqkvw, gvw=gvw, qscale=qscale,
                             kv_transposed=kv_transposed)
    row = lambda w: pl.BlockSpec((None, tm, w), lambda b, i: (b, i, 0))
    full = lambda a: pl.BlockSpec(a.shape, lambda b, i: (0,) * a.ndim)
    rs = lambda w, dt: jax.ShapeDtypeStruct((g, rows, w), dt)
    if kv_transposed:
        kv_spec = pl.BlockSpec((None, sbw, tm), lambda b, i: (b, 0, i))
        kv_shape = lambda dt: jax.ShapeDtypeStruct((g, sbw, rows), dt)
    else:
        kv_spec = row(sbw)
        kv_shape = lambda dt: rs(sbw, dt)
    out_shapes = (
        rs(sbw, BF16),
        kv_shape(BF16),
        kv_shape(BF16),
        kv_shape(F32),
        kv_shape(F32),
        rs(sbw, BF16),
        rs(gqkvw, F32),
        rs(gvw, BF16),
        rs(LANES, F32),
    )
    out_specs = (row(sbw), kv_spec, kv_spec, kv_spec, kv_spec, row(sbw), row(gqkvw), row(gvw), row(LANES))
    return pl.pallas_call(
        kern, grid=(g, rows // tm),
        in_specs=[row(d), full(norm_w), full(w_main), full(w_kvt), full(w_tail)],
        out_specs=out_specs, out_shape=out_shapes,
        compiler_params=_cparams(("parallel", "parallel")), name="proj0",
    )(x, norm_w, w_main, w_kvt, w_tail)


def _neg_suffix_ones(tk):
    j = lax.broadcasted_iota(jnp.int32, (tk, 2 * tk), 0)
    s = lax.broadcasted_iota(jnp.int32, (tk, 2 * tk), 1)
    return jnp.where((j >= s) | (s >= tk), -1.0, 0.0).astype(BF16)


def _sb_tile(z, r_rep, negu2, mask):
    tk = z.shape[1]
    sp = _softplus2(z)
    if mask is not None:
        sp = jnp.where(mask, sp, 0.0)
    res = jnp.dot(sp.astype(BF16), negu2, preferred_element_type=F32)
    t = z + res[:, :tk] + r_rep
    a = jnp.exp2(t)
    if mask is not None:
        a = jnp.where(mask, a, 0.0)
    return a.astype(BF16), r_rep + res[:, tk:]


def _sb_prompt_kernel(bias_ref, q_ref, k_ref, v_ref, g_ref, negu_ref, o_ref, acc_ref, r_ref, *, tq):
    p = pl.program_id(1)
    i = pl.program_id(2)
    hd = SB_HEAD_DIM
    q = q_ref[...]
    lane = lax.broadcasted_iota(jnp.int32, q.shape, 1)
    zero = jnp.zeros_like(q)
    q2 = jnp.concatenate([jnp.where(lane < hd, q, zero), jnp.where(lane >= hd, q, zero)], axis=0)
    b_lo = bias_ref[2 * p] * LOG2E
    b_hi = bias_ref[2 * p + 1] * LOG2E
    negu2 = negu_ref[...]

    def logits(j):
        kt = k_ref[:, pl.ds(pl.multiple_of(j * tq, tq), tq)]
        s = jnp.dot(q2, kt, preferred_element_type=F32)
        return jnp.concatenate([s[:tq] + b_lo, s[tq:] + b_hi], axis=0)

    def attend(a, j):
        return _mm_nt(a, v_ref[:, pl.ds(pl.multiple_of(j * tq, tq), tq)])

    row = lax.broadcasted_iota(jnp.int32, (2 * tq, tq), 0)
    col = lax.broadcasted_iota(jnp.int32, (2 * tq, tq), 1)
    diag_mask = col < jnp.where(row >= tq, row - tq, row)
    a, r = _sb_tile(logits(i), jnp.zeros((2 * tq, tq), F32), negu2, diag_mask)
    acc_ref[...] = attend(a, i)
    r_ref[...] = r

    def body(jj, carry):
        j = i - 1 - jj
        a, r = _sb_tile(logits(j), r_ref[...], negu2, None)
        acc_ref[...] += attend(a, j)
        r_ref[...] = r
        return carry

    lax.fori_loop(0, i, body, 0)
    acc = acc_ref[...]
    lane_o = lax.broadcasted_iota(jnp.int32, (tq, 2 * hd), 1)
    o = jnp.where(lane_o < hd, acc[:tq], acc[tq:])
    o_ref[...] = (o * g_ref[...].astype(F32)).astype(BF16)


def _sb_prompt(bias, q, kt, vt, gate, *, tq):
    n, t, w = q.shape
    assert w % LANES == 0 and t % tq == 0 and tq == LANES
    negu2 = _neg_suffix_ones(tq)
    blk = pl.BlockSpec((None, tq, LANES), lambda b, p, i: (b, i, p))
    seq = pl.BlockSpec((None, LANES, t), lambda b, p, i: (b, p, 0))
    return pl.pallas_call(
        functools.partial(_sb_prompt_kernel, tq=tq),
        grid=(n, w // LANES, t // tq),
        in_specs=[pl.BlockSpec(memory_space=pltpu.SMEM), blk, seq, seq, blk,
                  pl.BlockSpec(negu2.shape, lambda b, p, i: (0, 0))],
        out_specs=blk,
        out_shape=jax.ShapeDtypeStruct((n, t, w), BF16),
        scratch_shapes=[pltpu.VMEM((2 * tq, LANES), F32), pltpu.VMEM((2 * tq, tq), F32)],
        compiler_params=_cparams(("parallel", "parallel", "arbitrary")), name="sb_prompt",
    )(bias, q, kt, vt, gate, negu2)


def _sb_paged_kernel(pt_ref, qbd_ref, kn_ref, vn_ref, g_ref, bias_ref, negu_ref, *rest, pp, heads, tnew):
    k_refs = rest[:pp]
    v_refs = rest[pp:2 * pp]
    o_ref, acc_ref, r_ref = rest[2 * pp:]
    g = pl.program_id(1)
    qbd = qbd_ref[...]
    bias = bias_ref[...]
    negu2 = negu_ref[...]
    rows, page = bias.shape

    @pl.when(g == 0)
    def _():
        z = _mm_nt(qbd, kn_ref[...]) + bias
        row = lax.broadcasted_iota(jnp.int32, (rows, page), 0)
        col = lax.broadcasted_iota(jnp.int32, (rows, page), 1)
        mask = col < (row % tnew)
        a, r = _sb_tile(z, jnp.zeros((rows, page), F32), negu2, mask)
        acc_ref[...] = jnp.dot(a, vn_ref[...], preferred_element_type=F32)
        r_ref[...] = r

    for kr, vr in zip(k_refs, v_refs):
        z = _mm(qbd, kr[...]) + bias
        a, r = _sb_tile(z, r_ref[...], negu2, None)
        acc_ref[...] += _mm_nt(a, vr[...])
        r_ref[...] = r

    @pl.when(g == pl.num_programs(1) - 1)
    def _():
        acc = acc_ref[...]
        lane_h = lax.broadcasted_iota(jnp.int32, (tnew, acc.shape[1]), 1) // SB_HEAD_DIM
        o = jnp.zeros((tnew, acc.shape[1]), F32)
        for h in range(heads):
            o = o + jnp.where(lane_h == h, acc[h * tnew:(h + 1) * tnew], 0.0)
        o_ref[...] = (o * g_ref[...].astype(F32)).astype(BF16)


def _sb_paged(page_table, bias, q, k_new, v_new, gate, kpool, vpool, *, pp):
    n, tnew, w = q.shape
    heads = w // SB_HEAD_DIM
    npages = page_table.shape[1]
    page = kpool.shape[2]
    assert npages % pp == 0 and tnew <= page and tnew % SUBLANES == 0
    rows = heads * tnew
    head_of_lane = jnp.arange(w) // SB_HEAD_DIM
    sel = (head_of_lane[None, :] == jnp.arange(heads)[:, None])
    qbd = jnp.where(sel[None, :, None, :], q[:, None, :, :], 0).reshape(n, rows, w)
    pad = ((0, 0), (0, page - tnew), (0, 0))
    kn = jnp.pad(k_new, pad)
    vn = jnp.pad(v_new, pad)
    bias_rep = jnp.broadcast_to(jnp.repeat(bias.astype(F32) * LOG2E, tnew)[:, None], (rows, page))
    negu2 = _neg_suffix_ones(page)

    def page_map(b, g, pt, *, r):
        return (pt[b, npages - 1 - (g * pp + r)], 0, 0)

    per_seq = lambda shape: pl.BlockSpec((None,) + shape, lambda b, g, pt: (b, 0, 0))
    const = lambda a: pl.BlockSpec(a.shape, lambda b, g, pt: (0,) * a.ndim)
    page_specs = [pl.BlockSpec((None, w, page), functools.partial(page_map, r=r)) for r in range(pp)]
    grid_spec = pltpu.PrefetchScalarGridSpec(
        num_scalar_prefetch=1, grid=(n, npages // pp),
        in_specs=[per_seq((rows, w)), per_seq((page, w)), per_seq((page, w)), per_seq((tnew, w)),
                  const(bias_rep), const(negu2)] + page_specs + page_specs,
        out_specs=per_seq((tnew, w)),
        scratch_shapes=[pltpu.VMEM((rows, w), F32), pltpu.VMEM((rows, page), F32)],
    )
    return pl.pallas_call(
        functools.partial(_sb_paged_kernel, pp=pp, heads=heads, tnew=tnew),
        grid_spec=grid_spec,
        out_shape=jax.ShapeDtypeStruct((n, tnew, w), BF16),
        compiler_params=_cparams(("parallel", "arbitrary")), name="sb_paged",
    )(page_table, qbd, kn, vn, gate, bias_rep, negu2, *([kpool] * pp), *([vpool] * pp))


def _causal_conv_rows(x, prev8, w, width):
    rows = x.shape[0]
    xx = jnp.concatenate([prev8, x], axis=0)
    y = x * w[width - 1:width, :]
    for s in range(1, width):
        y = y + pltpu.roll(xx, s, 0)[SUBLANES:SUBLANES + rows] * w[width - 1 - s:width - s, :]
    return y


def _unit_lower_inverse(m):
    c = m.shape[0]
    r = lax.broadcasted_iota(jnp.int32, (c, c), 0)
    s = lax.broadcasted_iota(jnp.int32, (c, c), 1)
    inv = jnp.where(r == s, 1.0, 0.0) - m
    p = m
    for _ in range(max(0, int(math.log2(c)) - 1)):
        p = _mm3(p, p)
        inv = inv + _mm3(inv, p)
    return inv


def _gdn_kernel(x_ref, tail_ref, gate_ref, prev_ref, s0_ref, cw_ref, gp_ref, onw_ref,
                o_ref, sout_ref, s_ref, carry_ref, *, chunk):
    b = pl.program_id(1)
    heads, dk = GDN_HEADS, GDN_DK
    tb = x_ref.shape[0]
    qkw = heads * dk

    @pl.when(b == 0)
    def _():
        s_ref[...] = s0_ref[...]
        carry_ref[...] = prev_ref[...]

    x = x_ref[...]
    y = _silu(_causal_conv_rows(x, carry_ref[...], cw_ref[...], GDN_CONV))
    carry_ref[...] = x[tb - SUBLANES:, :]

    tl = tail_ref[...]
    gp = gp_ref[...]
    beta_all = 1.0 / (1.0 + jnp.exp(-tl))
    xg = tl + gp[1:2, :]
    g_all = -jnp.exp(gp[0:1, :]) * (jnp.maximum(xg, 0.0) + jnp.log(1.0 + jnp.exp(-jnp.abs(xg))))

    ci = lax.broadcasted_iota(jnp.int32, (chunk, chunk), 0)
    cj = lax.broadcasted_iota(jnp.int32, (chunk, chunk), 1)
    tril_incl = jnp.where(ci >= cj, 1.0, 0.0)
    eye = jnp.where(ci == cj, 1.0, 0.0)
    onw = onw_ref[...]

    for c in range(tb // chunk):
        rows = slice(c * chunk, (c + 1) * chunk)
        gc_all = _mm_exact_lhs(tril_incl, g_all[rows])
        for h in range(heads):
            qh = y[rows, h * dk:(h + 1) * dk]
            kh = y[rows, qkw + h * dk:qkw + (h + 1) * dk]
            vh = y[rows, 2 * qkw + h * dk:2 * qkw + (h + 1) * dk]
            qh = qh * (lax.rsqrt(jnp.sum(qh * qh, -1, keepdims=True) + NORM_EPS) * dk ** -0.5)
            kh = kh * lax.rsqrt(jnp.sum(kh * kh, -1, keepdims=True) + NORM_EPS)
            beta = beta_all[rows, h:h + 1]
            gc = gc_all[:, heads + h:heads + h + 1]
            gc_col = jnp.broadcast_to(gc, (chunk, chunk))
            gc_row = _mm_exact_lhs(jnp.ones((chunk, chunk), F32), eye * gc_col)
            diff = gc_col - gc_row
            decay = jnp.exp(jnp.where(ci >= cj, diff, 0.0))
            kb = kh * beta
            m = jnp.where(ci > cj, _mm_nt(kb, kh) * decay, 0.0)
            t_inv = _unit_lower_inverse(m)
            egc = jnp.exp(gc)
            uw = _mm3(t_inv, jnp.concatenate([vh * beta, kb * egc], axis=1))
            u, wmat = uw[:, :dk], uw[:, dk:]
            qk = jnp.where(ci >= cj, _mm_nt(qh, kh) * decay, 0.0)
            s = s_ref[h]
            v_new = u - _mm(wmat, s)
            o = _mm(qh * egc, s) + _mm(qk, v_new)
            g_last = gc[chunk - 1:chunk, :]
            k_dec = kh * jnp.exp(g_last - gc)
            s_ref[h] = s * jnp.exp(g_last) + lax.dot_general(
                k_dec.astype(BF16), v_new.astype(BF16), (((0,), (0,)), ((), ())),
                preferred_element_type=F32)
            on = o * lax.rsqrt(jnp.mean(o * o, -1, keepdims=True) + NORM_EPS) * onw
            gate = gate_ref[rows, h * dk:(h + 1) * dk].astype(F32)
            o_ref[rows, h * dk:(h + 1) * dk] = (on * gate).astype(BF16)

    @pl.when(b == pl.num_programs(1) - 1)
    def _():
        sout_ref[...] = s_ref[...]


def _gdn(gqkv, tail, gate, prev8, s0, conv_w, gparams, out_norm_w, *, tb, chunk):
    n, t, w = gqkv.shape
    heads, dk = GDN_HEADS, GDN_DK
    assert t % tb == 0 and tb % chunk == 0 and tb % SUBLANES == 0
    tok = lambda width: pl.BlockSpec((None, tb, width), lambda i, b: (i, b, 0))
    const = lambda a: pl.BlockSpec(a.shape, lambda i, b: (0,) * a.ndim)
    state = pl.BlockSpec((None, heads, dk, dk), lambda i, b: (i, 0, 0, 0))
    return pl.pallas_call(
        functools.partial(_gdn_kernel, chunk=chunk),
        grid=(n, t // tb),
        in_specs=[tok(w), tok(LANES), tok(heads * dk),
                  pl.BlockSpec((None, SUBLANES, w), lambda i, b: (i, 0, 0)), state,
                  const(conv_w), const(gparams), const(out_norm_w)],
        out_specs=(tok(heads * dk), state),
        out_shape=(jax.ShapeDtypeStruct((n, t, heads * dk), BF16),
                   jax.ShapeDtypeStruct((n, heads, dk, dk), F32)),
        scratch_shapes=[pltpu.VMEM((heads, dk, dk), F32), pltpu.VMEM((SUBLANES, w), F32)],
        compiler_params=_cparams(("parallel", "arbitrary")), name="gdn",
    )(gqkv, tail, gate, prev8, s0, conv_w, gparams, out_norm_w)


def _tail_kernel(x_ref, ma_ref, mb_ref, prev_ref, wo_ref, nw_ref, win_ref, cw_ref, wout_ref, fnw_ref,
                 y_ref, last_ref, carry_ref, *, tt):
    b = pl.program_id(1)
    rows, d = x_ref.shape
    sa = ma_ref.shape[1]

    @pl.when(b == 0)
    def _():
        carry_ref[...] = prev_ref[...]

    h1 = x_ref[...] + (jnp.dot(ma_ref[...], wo_ref[:sa, :], preferred_element_type=F32)
                       + jnp.dot(mb_ref[...], wo_ref[sa:, :], preferred_element_type=F32))
    xn = _rms(h1, nw_ref[...]).astype(BF16)
    proj = lambda i: jnp.dot(xn, win_ref[:, i * d:(i + 1) * d], preferred_element_type=F32)
    cu = proj(2) * proj(0)
    cw = cw_ref[...]
    if tt == rows:
        yc = _causal_conv_rows(cu, carry_ref[...], cw, SC_CONV)
        carry_ref[...] = cu[rows - SUBLANES:, :]
        last_ref[...] = cu[rows - SUBLANES:, :]
    else:
        assert tt == SUBLANES
        prev = carry_ref[...]
        tok = lax.broadcasted_iota(jnp.int32, (rows, d), 0) % tt
        yc = cu * cw[SC_CONV - 1:SC_CONV, :]
        for s in range(1, SC_CONV):
            shifted = jnp.where(tok >= s, pltpu.roll(cu, s, 0), pltpu.roll(prev, (s - tt) % rows, 0))
            yc = yc + shifted * cw[SC_CONV - 1 - s:SC_CONV - s, :]
        last_ref[...] = cu
    m = (proj(1) * yc * _silu(proj(3))).astype(BF16)
    h2 = h1 + jnp.dot(m, wout_ref[...], preferred_element_type=F32)
    y_ref[...] = _rms(h2, fnw_ref[...])


def _tail(x, mix_a, mix_b, prev, w_out0, norm_w1, sc_w_in, sc_conv_w, sc_w_out, final_norm_w, *, tm, tt):
    g, r, d = x.shape
    assert r % tm == 0 and tm % tt == 0 and (tt == tm or (tt == SUBLANES and r == tm))
    nseq = tm // tt
    tok = lambda width: pl.BlockSpec((None, tm, width), lambda i, b: (i, b, 0))
    const = lambda a: pl.BlockSpec(a.shape, lambda i, b: (0,) * a.ndim)
    small = pl.BlockSpec((None, nseq * SUBLANES, d), lambda i, b: (i, 0, 0))
    return pl.pallas_call(
        functools.partial(_tail_kernel, tt=tt),
        grid=(g, r // tm),
        in_specs=[tok(d), tok(mix_a.shape[2]), tok(mix_b.shape[2]), small,
                  const(w_out0), const(norm_w1), const(sc_w_in), const(sc_conv_w), const(sc_w_out),
                  const(final_norm_w)],
        out_specs=(tok(d), small),
        out_shape=(jax.ShapeDtypeStruct((g, r, d), F32),
                   jax.ShapeDtypeStruct((g, nseq * SUBLANES, d), F32)),
        scratch_shapes=[pltpu.VMEM((nseq * SUBLANES, d), F32)],
        compiler_params=_cparams(("parallel", "arbitrary")), name="tail",
    )(x, mix_a, mix_b, prev, w_out0, norm_w1, sc_w_in, sc_conv_w, sc_w_out, final_norm_w)


def _pad_front_rows(buf, rows):
    return jnp.pad(buf, ((0, 0), (rows - buf.shape[1], 0), (0, 0)))


def _trunk(x, weights, sb_fn, gdn_state, gdn_buf, sc_buf, *, kv_transposed, tm_proj, gdn_tb, gdn_chunk,
           tail_tm, tail_tt):
    (norm_w, w_main, w_kvt, w_tail, conv_w, gparams, out_norm_w, w_out0, sc_w_in, sc_conv_w, sc_w_out,
     final_norm_w) = weights
    n, t, d = x.shape
    sbw = w_out0.shape[0] // 2
    heads = sbw // SB_HEAD_DIM
    gvw = GDN_HEADS * GDN_DK
    gqkvw = conv_w.shape[1]
    xp = x if kv_transposed else x.reshape(1, n * t, d)
    q, kb, vb, kf, vf, sg, gqkv, gg, tail = _proj0(
        xp, norm_w[0:1], w_main, w_kvt, w_tail, sbw=sbw, gqkvw=gqkvw, gvw=gvw, tm=tm_proj,
        kv_transposed=kv_transposed)
    r3 = lambda a: a.reshape(n, t, a.shape[-1])
    if kv_transposed:
        new_kv = lambda a: jnp.transpose(a.reshape(n, heads, SB_HEAD_DIM, t), (0, 3, 1, 2))[None]
        mix_a = sb_fn(q, kb, vb, sg)
    else:
        new_kv = lambda a: a.reshape(1, n, t, heads, SB_HEAD_DIM)
        mix_a = sb_fn(r3(q), r3(kb), r3(vb), r3(sg))
    new_k, new_v = new_kv(kf), new_kv(vf)
    gqkv3 = r3(gqkv)
    mix_b, new_state = _gdn(gqkv3, r3(tail), r3(gg), _pad_front_rows(gdn_buf, SUBLANES), gdn_state,
                            conv_w, gparams, out_norm_w, tb=gdn_tb, chunk=gdn_chunk)
    prev_sc = _pad_front_rows(sc_buf, SUBLANES)
    if tail_tt == tail_tm:
        groups = (n, t)
    else:
        groups = (1, n * t)
        prev_sc = prev_sc.reshape(1, n * SUBLANES, d)
    shp = lambda a: a.reshape(groups + (a.shape[-1],))
    y, last = _tail(shp(x), shp(mix_a), shp(mix_b), prev_sc, w_out0, norm_w[1:2], sc_w_in, sc_conv_w,
                    sc_w_out, final_norm_w, tm=tail_tm, tt=tail_tt)
    new_gbuf = gqkv3[:, t - (GDN_CONV - 1):, :][None]
    new_scbuf = last.reshape(n, SUBLANES, d)[:, SUBLANES - (SC_CONV - 1):, :][None]
    return y.reshape(n, t, d), new_k, new_v, new_state[None], new_gbuf, new_scbuf


def kernel(x_prompt, x_sample, cache_sb_k, cache_sb_v, state_gdn, state_gdn_conv, state_sconv, page_table,
           norm_w, ab_w_in, ab_sb_bias, ab_conv_w, ab_a_log, ab_dt_bias, ab_out_norm_w, ab_w_out,
           sc_w_in, sc_conv_w, sc_w_out, final_norm_w):
    assert ab_w_in.shape[0] == 1 and sc_w_in.shape[0] == 1, "one layer of each kind"
    nb, t, d = x_prompt.shape
    ns, ts, _ = x_sample.shape
    heads = GDN_HEADS
    main_cols = ab_w_in.shape[2] - 2 * heads
    sbw = ab_w_out.shape[1] // 2
    w_main = ab_w_in[0, :, :main_cols].astype(BF16)
    w_kvt = ab_w_in[0, :, sbw:3 * sbw].T.astype(BF16)
    w_tail = jnp.pad(ab_w_in[0, :, main_cols:], ((0, 0), (0, LANES - 2 * heads))).astype(BF16)
    gparams = jnp.zeros((SUBLANES, LANES), F32)
    gparams = gparams.at[0, heads:2 * heads].set(ab_a_log[0]).at[1, heads:2 * heads].set(ab_dt_bias[0])
    weights = (norm_w, w_main, w_kvt, w_tail, ab_conv_w[0], gparams, ab_out_norm_w[0:1],
               ab_w_out[0].astype(BF16), sc_w_in[0].astype(BF16), sc_conv_w[0], sc_w_out[0].astype(BF16),
               final_norm_w[None])
    bias = ab_sb_bias[0]

    zeros = lambda *s: jnp.zeros(s, F32)
    tm = min(256, t)
    y_p, k_p, v_p, s_p, gb_p, sc_p = _trunk(
        x_prompt, weights, functools.partial(_sb_prompt, bias, tq=LANES),
        zeros(nb, heads, GDN_DK, GDN_DK), zeros(nb, GDN_CONV - 1, ab_conv_w.shape[2]),
        zeros(nb, SC_CONV - 1, d),
        kv_transposed=True, tm_proj=tm, gdn_tb=min(128, t), gdn_chunk=min(64, t), tail_tm=tm, tail_tt=tm)

    n_pool, page = cache_sb_k.shape[1], cache_sb_k.shape[2]
    pool_t = lambda c: jnp.transpose(c, (0, 1, 3, 4, 2)).reshape(n_pool, -1, page)
    kpool = pool_t(cache_sb_k)
    vpool = pool_t(cache_sb_v)
    sb_sample = lambda q, kb, vb, sg: _sb_paged(page_table, bias, q, kb, vb, sg, kpool, vpool,
                                                pp=min(8, page_table.shape[1]))
    y_s, k_s, v_s, s_s, gb_s, sc_s = _trunk(
        x_sample, weights, sb_sample, state_gdn[0], state_gdn_conv[0], state_sconv[0],
        kv_transposed=False, tm_proj=ns * ts, gdn_tb=ts, gdn_chunk=ts, tail_tm=ns * ts, tail_tt=ts)
    return (y_p, y_s, k_p, v_p, k_s, v_s, s_p, s_s, gb_p, gb_s, sc_p, sc_s)
```

```python
import functools
import math

import jax
import jax.numpy as jnp
from jax import lax
from jax.experimental import pallas as pl
from jax.experimental.pallas import tpu as pltpu

F32 = jnp.float32
BF16 = jnp.bfloat16
LOG2E = 1.4426950408889634
NORM_EPS = 1e-6
LANES = 128
SUBLANES = 8
VMEM_LIMIT_BYTES = 56 * 1024 * 1024

SB_HEAD_DIM = 64
GDN_HEADS = 4
GDN_DK = 128
GDN_CONV = 4
SC_CONV = 3


def _cparams(sem):
    return pltpu.CompilerParams(dimension_semantics=sem, vmem_limit_bytes=VMEM_LIMIT_BYTES)


def _silu(x):
    return x * (1.0 / (1.0 + jnp.exp(-x)))


def _rms(x, w):
    return x * lax.rsqrt(jnp.mean(x * x, axis=-1, keepdims=True) + NORM_EPS) * w


def _mm(a, b):
    return jnp.dot(a.astype(BF16), b.astype(BF16), preferred_element_type=F32)


def _mm_nt(a, b):
    return lax.dot_general(a.astype(BF16), b.astype(BF16), (((1,), (1,)), ((), ())),
                           preferred_element_type=F32)


def _split3(x):
    hi = x.astype(BF16)
    r = x - hi.astype(F32)
    mid = r.astype(BF16)
    lo = (r - mid.astype(F32)).astype(BF16)
    return hi, mid, lo


def _mm3(a, b):
    ah, am, _ = _split3(a)
    bh, bm, _ = _split3(b)
    d = functools.partial(jnp.dot, preferred_element_type=F32)
    return d(ah, bh) + (d(ah, bm) + d(am, bh))


def _mm_exact_lhs(a01, b):
    a = a01.astype(BF16)
    bh, bm, bl = _split3(b)
    d = functools.partial(jnp.dot, preferred_element_type=F32)
    return d(a, bh) + (d(a, bm) + d(a, bl))


def _softplus2(z):
    neg_abs = pltpu.bitcast(pltpu.bitcast(z, jnp.uint32) | jnp.uint32(0x80000000), F32)
    return jnp.maximum(z, 0.0) + jnp.log2(1.0 + jnp.exp2(neg_abs))


def _proj0_kernel(x_ref, nw_ref, w_ref, wkvt_ref, wt_ref,
                  q_ref, kb_ref, vb_ref, kf_ref, vf_ref, sg_ref, gqkv_ref, gg_ref, tail_ref,
                  *, sbw, gqkvw, gvw, qscale, kv_transposed):
    x = x_ref[...]
    xn = _rms(x, nw_ref[...]).astype(BF16)

    def proj(lo, width):
        return jnp.dot(xn, w_ref[:, lo:lo + width], preferred_element_type=F32)

    q_ref[...] = (proj(0, sbw) * qscale).astype(BF16)
    if kv_transposed:
        nt = lambda wt: lax.dot_general(wt, xn, (((1,), (1,)), ((), ())), preferred_element_type=F32)
        k = nt(wkvt_ref[:sbw, :])
        v = nt(wkvt_ref[sbw:, :])
    else:
        k = proj(sbw, sbw)
        v = proj(2 * sbw, sbw)
    kf_ref[...] = k
    kb_ref[...] = k.astype(BF16)
    vf_ref[...] = v
    vb_ref[...] = v.astype(BF16)
    sg_ref[...] = _silu(proj(3 * sbw, sbw)).astype(BF16)
    gqkv_ref[...] = proj(4 * sbw, gqkvw)
    gg_ref[...] = _silu(proj(4 * sbw + gqkvw, gvw)).astype(BF16)
    tail_ref[...] = jnp.dot(xn, wt_ref[...], preferred_element_type=F32)


def _proj0(x, norm_w, w_main, w_kvt, w_tail, *, sbw, gqkvw, gvw, tm, kv_transposed):
    g, rows, d = x.shape
    assert rows % tm == 0
    qscale = LOG2E * SB_HEAD_DIM ** -0.5
    kern = functools.partial(_proj0_kernel, sbw=sbw, gqkvw=gqkvw, gvw=gvw, qscale=qscale,
                             kv_transposed=kv_transposed)
    row = lambda w: pl.BlockSpec((None, tm, w), lambda b, i: (b, i, 0))
    full = lambda a: pl.BlockSpec(a.shape, lambda b, i: (0,) * a.ndim)
    rs = lambda w, dt: jax.ShapeDtypeStruct((g, rows, w), dt)
    if kv_transposed:
        kv_spec = pl.BlockSpec((None, sbw, tm), lambda b, i: (b, 0, i))
        kv_shape = lambda dt: jax.ShapeDtypeStruct((g, sbw, rows), dt)
    else:
        kv_spec = row(sbw)
        kv_shape = lambda dt: rs(sbw, dt)
    out_shapes = (
        rs(sbw, BF16),
        kv_shape(BF16),
        kv_shape(BF16),
        kv_shape(F32),
        kv_shape(F32),
        rs(sbw, BF16),
        rs(gqkvw, F32),
        rs(gvw, BF16),
        rs(LANES, F32),
    )
    out_specs = (row(sbw), kv_spec, kv_spec, kv_spec, kv_spec, row(sbw), row(gqkvw), row(gvw), row(LANES))
    return pl.pallas_call(
        kern, grid=(g, rows // tm),
        in_specs=[row(d), full(norm_w), full(w_main), full(w_kvt), full(w_tail)],
        out_specs=out_specs, out_shape=out_shapes,
        compiler_params=_cparams(("parallel", "parallel")), name="proj0",
    )(x, norm_w, w_main, w_kvt, w_tail)


def _neg_suffix(tk):
    j = lax.broadcasted_iota(jnp.int32, (tk, tk), 0)
    s = lax.broadcasted_iota(jnp.int32, (tk, tk), 1)
    return jnp.where(j >= s, -1.0, 0.0).astype(BF16)


def _sb_tile(z, r_rep, negu, mask):
    tk = z.shape[1]
    sp = _softplus2(z)
    if mask is not None:
        sp = jnp.where(mask, sp, 0.0)
    incl = jnp.dot(sp.astype(BF16), negu, preferred_element_type=F32)
    r_all = r_rep if tk == LANES else jnp.concatenate([r_rep] * (tk // LANES), axis=1)
    a = jnp.exp2(z + incl + r_all)
    if mask is not None:
        a = jnp.where(mask, a, 0.0)
    return a.astype(BF16), r_rep + jnp.broadcast_to(incl[:, 0:1], r_rep.shape)


BIAS_PIECES = 3


def _sb_prompt_kernel(bias_ref, q_ref, k_ref, v_ref, g_ref, negu_ref, o_ref, kaug_ref, acc_ref, r_ref,
                      *, tq, gp):
    pg = pl.program_id(1)
    i = pl.program_id(2)
    hd = SB_HEAD_DIM

    @pl.when(i == 0)
    def _():
        rown = lax.broadcasted_iota(jnp.int32, (LANES, k_ref.shape[1]), 0)
        ones_rows = jnp.where(rown < BIAS_PIECES, 1.0, 0.0).astype(BF16)
        for g in range(gp):
            kaug_ref[g, :LANES, :] = k_ref[g * LANES:(g + 1) * LANES, :]
            kaug_ref[g, LANES:, :] = ones_rows

    lane = lax.broadcasted_iota(jnp.int32, (tq, LANES), 1)
    zero = jnp.zeros((tq, LANES), BF16)

    def bias_lanes(b):
        pieces = _split3(jnp.full((tq, LANES), b * LOG2E, F32))
        out = jnp.zeros((tq, LANES), F32)
        for n, piece in enumerate(pieces):
            out = jnp.where(lane == n, piece.astype(F32), out)
        return out.astype(BF16)

    def stacked_q(g):
        q = q_ref[:, g * LANES:(g + 1) * LANES]
        h0 = 2 * (pg * gp + g)
        return jnp.concatenate(
            [jnp.concatenate([jnp.where(lane < hd, q, zero), bias_lanes(bias_ref[h0])], axis=1),
             jnp.concatenate([jnp.where(lane >= hd, q, zero), bias_lanes(bias_ref[h0 + 1])], axis=1)],
            axis=0)

    q2 = [stacked_q(g) for g in range(gp)]
    negu = negu_ref[...]

    def span(j):
        return pl.ds(pl.multiple_of(j * tq, tq), tq)

    def tile(g, j, r_rep, mask):
        z = jnp.dot(q2[g], kaug_ref[g, :, span(j)], preferred_element_type=F32)
        a, r = _sb_tile(z, r_rep, negu, mask)
        return _mm_nt(a, v_ref[g * LANES:(g + 1) * LANES, span(j)]), r

    row = lax.broadcasted_iota(jnp.int32, (2 * tq, tq), 0)
    col = lax.broadcasted_iota(jnp.int32, (2 * tq, tq), 1)
    diag_mask = col < jnp.where(row >= tq, row - tq, row)
    for g in range(gp):
        pv, r = tile(g, i, jnp.zeros((2 * tq, LANES), F32), diag_mask)
        acc_ref[g] = pv
        r_ref[g] = r

    def body(jj, carry):
        j = i - 1 - jj
        for g in range(gp):
            pv, r = tile(g, j, r_ref[g], None)
            acc_ref[g] += pv
            r_ref[g] = r
        return carry

    lax.fori_loop(0, i, body, 0)
    lane_o = lax.broadcasted_iota(jnp.int32, (tq, 2 * hd), 1)
    for g in range(gp):
        acc = acc_ref[g]
        o = jnp.where(lane_o < hd, acc[:tq], acc[tq:])
        cols = slice(g * LANES, (g + 1) * LANES)
        o_ref[:, cols] = (o * g_ref[:, cols].astype(F32)).astype(BF16)


def _sb_prompt(bias, q, kt, vt, gate, *, tq, gp):
    n, t, w = q.shape
    gw = gp * LANES
    assert w % gw == 0 and t % tq == 0 and tq % LANES == 0
    negu = _neg_suffix(tq)
    blk = pl.BlockSpec((None, tq, gw), lambda b, p, i: (b, i, p))
    seq = pl.BlockSpec((None, gw, t), lambda b, p, i: (b, p, 0))
    return pl.pallas_call(
        functools.partial(_sb_prompt_kernel, tq=tq, gp=gp),
        grid=(n, w // gw, t // tq),
        in_specs=[pl.BlockSpec(memory_space=pltpu.SMEM), blk, seq, seq, blk,
                  pl.BlockSpec(negu.shape, lambda b, p, i: (0, 0))],
        out_specs=blk,
        out_shape=jax.ShapeDtypeStruct((n, t, w), BF16),
        scratch_shapes=[pltpu.VMEM((gp, 2 * LANES, t), BF16), pltpu.VMEM((gp, 2 * tq, LANES), F32),
                        pltpu.VMEM((gp, 2 * tq, LANES), F32)],
        compiler_params=_cparams(("parallel", "parallel", "arbitrary")), name="sb_prompt",
    )(bias, q, kt, vt, gate, negu)


def _sb_paged_kernel(pt_ref, qbd_ref, kn_ref, vn_ref, g_ref, bias_ref, negu_ref, *rest, pp, heads, tnew):
    k_refs = rest[:pp]
    v_refs = rest[pp:2 * pp]
    o_ref, acc_ref, r_ref = rest[2 * pp:]
    g = pl.program_id(1)
    qbd = qbd_ref[...]
    bias = bias_ref[...]
    negu = negu_ref[...]
    rows, page = bias.shape

    @pl.when(g == 0)
    def _():
        z = _mm_nt(qbd, kn_ref[...]) + bias
        row = lax.broadcasted_iota(jnp.int32, (rows, page), 0)
        col = lax.broadcasted_iota(jnp.int32, (rows, page), 1)
        mask = col < (row % tnew)
        a, r = _sb_tile(z, jnp.zeros((rows, page), F32), negu[:page, :page], mask)
        acc_ref[...] = _mm_nt(vn_ref[...], a)
        r_ref[...] = r

    r = r_ref[...]
    acc = acc_ref[...]
    bias2 = jnp.concatenate([bias, bias], axis=1)
    pair = lambda refs, w: jnp.concatenate([refs[2 * w][...].astype(BF16), refs[2 * w + 1][...].astype(BF16)],
                                           axis=1)
    for w in reversed(range(pp // 2)):
        z = jnp.dot(qbd, pair(k_refs, w), preferred_element_type=F32) + bias2
        a, r = _sb_tile(z, r, negu, None)
        acc = acc + _mm_nt(pair(v_refs, w), a)
    acc_ref[...] = acc
    r_ref[...] = r

    @pl.when(g == pl.num_programs(1) - 1)
    def _():
        width = acc_ref.shape[0]
        eye = (lax.broadcasted_iota(jnp.int32, (rows, rows), 0)
               == lax.broadcasted_iota(jnp.int32, (rows, rows), 1)).astype(F32).astype(BF16)
        acc_t = sum(_mm_nt(eye, piece) for piece in _split3(acc_ref[...]))
        lane_h = lax.broadcasted_iota(jnp.int32, (tnew, width), 1) // SB_HEAD_DIM
        o = jnp.zeros((tnew, width), F32)
        for h in range(heads):
            o = o + jnp.where(lane_h == h, acc_t[h * tnew:(h + 1) * tnew], 0.0)
        o_ref[...] = (o * g_ref[...].astype(F32)).astype(BF16)


def _sb_paged(page_table, bias, q, k_new, v_new, gate, kpool, vpool, *, pp):
    n, tnew, w = q.shape
    heads = w // SB_HEAD_DIM
    npages = page_table.shape[1]
    page = kpool.shape[2]
    assert npages % pp == 0 and pp % 2 == 0 and tnew <= page and tnew % SUBLANES == 0
    rows = heads * tnew
    head_of_lane = jnp.arange(w) // SB_HEAD_DIM
    sel = (head_of_lane[None, :] == jnp.arange(heads)[:, None])
    qbd = jnp.where(sel[None, :, None, :], q[:, None, :, :], 0).reshape(n, rows, w)
    pad = ((0, 0), (0, page - tnew), (0, 0))
    kn = jnp.pad(k_new, pad)
    vn = jnp.swapaxes(jnp.pad(v_new, pad), 1, 2)
    bias_rep = jnp.broadcast_to(jnp.repeat(bias.astype(F32) * LOG2E, tnew)[:, None], (rows, page))
    negu = _neg_suffix(2 * page)

    def page_map(b, g, pt, *, r):
        return (pt[b, npages - (g + 1) * pp + r], 0, 0)

    per_seq = lambda shape: pl.BlockSpec((None,) + shape, lambda b, g, pt: (b, 0, 0))
    const = lambda a: pl.BlockSpec(a.shape, lambda b, g, pt: (0,) * a.ndim)
    page_specs = [pl.BlockSpec((None, w, page), functools.partial(page_map, r=r)) for r in range(pp)]
    grid_spec = pltpu.PrefetchScalarGridSpec(
        num_scalar_prefetch=1, grid=(n, npages // pp),
        in_specs=[per_seq((rows, w)), per_seq((page, w)), per_seq((w, page)), per_seq((tnew, w)),
                  const(bias_rep), const(negu)] + page_specs + page_specs,
        out_specs=per_seq((tnew, w)),
        scratch_shapes=[pltpu.VMEM((w, rows), F32), pltpu.VMEM((rows, page), F32)],
    )
    return pl.pallas_call(
        functools.partial(_sb_paged_kernel, pp=pp, heads=heads, tnew=tnew),
        grid_spec=grid_spec,
        out_shape=jax.ShapeDtypeStruct((n, tnew, w), BF16),
        compiler_params=_cparams(("parallel", "arbitrary")), name="sb_paged",
    )(page_table, qbd, kn, vn, gate, bias_rep, negu, *([kpool] * pp), *([vpool] * pp))


def _causal_conv_rows(x, prev8, w, width):
    rows = x.shape[0]
    xx = jnp.concatenate([prev8, x], axis=0)
    y = x * w[width - 1:width, :]
    for s in range(1, width):
        y = y + pltpu.roll(xx, s, 0)[SUBLANES:SUBLANES + rows] * w[width - 1 - s:width - s, :]
    return y


def _unit_lower_inverse(m):
    c = m.shape[0]
    r = lax.broadcasted_iota(jnp.int32, (c, c), 0)
    s = lax.broadcasted_iota(jnp.int32, (c, c), 1)
    inv = jnp.where(r == s, 1.0, 0.0) - m
    p = m
    for _ in range(max(0, int(math.log2(c)) - 1)):
        p = _mm3(p, p)
        inv = inv + _mm3(inv, p)
    return inv


def _gdn_kernel(x_ref, tail_ref, gate_ref, prev_ref, s0_ref, cw_ref, gp_ref, onw_ref,
                o_ref, sout_ref, s_ref, carry_ref, *, chunk):
    b = pl.program_id(1)
    heads, dk = GDN_HEADS, GDN_DK
    tb = x_ref.shape[0]
    qkw = heads * dk

    @pl.when(b == 0)
    def _():
        s_ref[...] = s0_ref[...]
        carry_ref[...] = prev_ref[...]

    x = x_ref[...]
    y = _silu(_causal_conv_rows(x, carry_ref[...], cw_ref[...], GDN_CONV))
    carry_ref[...] = x[tb - SUBLANES:, :]

    tl = tail_ref[...]
    gp = gp_ref[...]
    beta_all = 1.0 / (1.0 + jnp.exp(-tl))
    xg = tl + gp[1:2, :]
    g_all = -jnp.exp(gp[0:1, :]) * (jnp.maximum(xg, 0.0) + jnp.log(1.0 + jnp.exp(-jnp.abs(xg))))

    ci = lax.broadcasted_iota(jnp.int32, (chunk, chunk), 0)
    cj = lax.broadcasted_iota(jnp.int32, (chunk, chunk), 1)
    tril_incl = jnp.where(ci >= cj, 1.0, 0.0)
    eye = jnp.where(ci == cj, 1.0, 0.0)
    onw = onw_ref[...]

    for c in range(tb // chunk):
        rows = slice(c * chunk, (c + 1) * chunk)
        gc_all = _mm_exact_lhs(tril_incl, g_all[rows])
        for h in range(heads):
            qh = y[rows, h * dk:(h + 1) * dk]
            kh = y[rows, qkw + h * dk:qkw + (h + 1) * dk]
            vh = y[rows, 2 * qkw + h * dk:2 * qkw + (h + 1) * dk]
            qh = qh * (lax.rsqrt(jnp.sum(qh * qh, -1, keepdims=True) + NORM_EPS) * dk ** -0.5)
            kh = kh * lax.rsqrt(jnp.sum(kh * kh, -1, keepdims=True) + NORM_EPS)
            beta = beta_all[rows, h:h + 1]
            gc = gc_all[:, heads + h:heads + h + 1]
            gc_col = jnp.broadcast_to(gc, (chunk, chunk))
            gc_row = _mm_exact_lhs(jnp.ones((chunk, chunk), F32), eye * gc_col)
            diff = gc_col - gc_row
            decay = jnp.exp(jnp.where(ci >= cj, diff, 0.0))
            kb = kh * beta
            m = jnp.where(ci > cj, _mm_nt(kb, kh) * decay, 0.0)
            t_inv = _unit_lower_inverse(m)
            egc = jnp.exp(gc)
            uw = _mm3(t_inv, jnp.concatenate([vh * beta, kb * egc], axis=1))
            u, wmat = uw[:, :dk], uw[:, dk:]
            qk = jnp.where(ci >= cj, _mm_nt(qh, kh) * decay, 0.0)
            s = s_ref[h]
            v_new = u - _mm(wmat, s)
            o = _mm(qh * egc, s) + _mm(qk, v_new)
            g_last = gc[chunk - 1:chunk, :]
            k_dec = kh * jnp.exp(g_last - gc)
            s_ref[h] = s * jnp.exp(g_last) + lax.dot_general(
                k_dec.astype(BF16), v_new.astype(BF16), (((0,), (0,)), ((), ())),
                preferred_element_type=F32)
            on = o * lax.rsqrt(jnp.mean(o * o, -1, keepdims=True) + NORM_EPS) * onw
            gate = gate_ref[rows, h * dk:(h + 1) * dk].astype(F32)
            o_ref[rows, h * dk:(h + 1) * dk] = (on * gate).astype(BF16)

    @pl.when(b == pl.num_programs(1) - 1)
    def _():
        sout_ref[...] = s_ref[...]


def _gdn(gqkv, tail, gate, prev8, s0, conv_w, gparams, out_norm_w, *, tb, chunk):
    n, t, w = gqkv.shape
    heads, dk = GDN_HEADS, GDN_DK
    assert t % tb == 0 and tb % chunk == 0 and tb % SUBLANES == 0
    tok = lambda width: pl.BlockSpec((None, tb, width), lambda i, b: (i, b, 0))
    const = lambda a: pl.BlockSpec(a.shape, lambda i, b: (0,) * a.ndim)
    state = pl.BlockSpec((None, heads, dk, dk), lambda i, b: (i, 0, 0, 0))
    return pl.pallas_call(
        functools.partial(_gdn_kernel, chunk=chunk),
        grid=(n, t // tb),
        in_specs=[tok(w), tok(LANES), tok(heads * dk),
                  pl.BlockSpec((None, SUBLANES, w), lambda i, b: (i, 0, 0)), state,
                  const(conv_w), const(gparams), const(out_norm_w)],
        out_specs=(tok(heads * dk), state),
        out_shape=(jax.ShapeDtypeStruct((n, t, heads * dk), BF16),
                   jax.ShapeDtypeStruct((n, heads, dk, dk), F32)),
        scratch_shapes=[pltpu.VMEM((heads, dk, dk), F32), pltpu.VMEM((SUBLANES, w), F32)],
        compiler_params=_cparams(("parallel", "arbitrary")), name="gdn",
    )(gqkv, tail, gate, prev8, s0, conv_w, gparams, out_norm_w)


def _tail_kernel(x_ref, ma_ref, mb_ref, prev_ref, wo_ref, nw_ref, win_ref, cw_ref, wout_ref, fnw_ref,
                 y_ref, last_ref, carry_ref, *, tt):
    b = pl.program_id(1)
    rows, d = x_ref.shape
    sa = ma_ref.shape[1]

    @pl.when(b == 0)
    def _():
        carry_ref[...] = prev_ref[...]

    h1 = x_ref[...] + (jnp.dot(ma_ref[...], wo_ref[:sa, :], preferred_element_type=F32)
                       + jnp.dot(mb_ref[...], wo_ref[sa:, :], preferred_element_type=F32))
    xn = _rms(h1, nw_ref[...]).astype(BF16)
    proj = lambda i: jnp.dot(xn, win_ref[:, i * d:(i + 1) * d], preferred_element_type=F32)
    cu = proj(2) * proj(0)
    cw = cw_ref[...]
    if tt == rows:
        yc = _causal_conv_rows(cu, carry_ref[...], cw, SC_CONV)
        carry_ref[...] = cu[rows - SUBLANES:, :]
        last_ref[...] = cu[rows - SUBLANES:, :]
    else:
        assert tt == SUBLANES
        prev = carry_ref[...]
        tok = lax.broadcasted_iota(jnp.int32, (rows, d), 0) % tt
        yc = cu * cw[SC_CONV - 1:SC_CONV, :]
        for s in range(1, SC_CONV):
            shifted = jnp.where(tok >= s, pltpu.roll(cu, s, 0), pltpu.roll(prev, (s - tt) % rows, 0))
            yc = yc + shifted * cw[SC_CONV - 1 - s:SC_CONV - s, :]
        last_ref[...] = cu
    m = (proj(1) * yc * _silu(proj(3))).astype(BF16)
    h2 = h1 + jnp.dot(m, wout_ref[...], preferred_element_type=F32)
    y_ref[...] = _rms(h2, fnw_ref[...])


def _tail(x, mix_a, mix_b, prev, w_out0, norm_w1, sc_w_in, sc_conv_w, sc_w_out, final_norm_w, *, tm, tt):
    g, r, d = x.shape
    assert r % tm == 0 and tm % tt == 0 and (tt == tm or (tt == SUBLANES and r == tm))
    nseq = tm // tt
    tok = lambda width: pl.BlockSpec((None, tm, width), lambda i, b: (i, b, 0))
    const = lambda a: pl.BlockSpec(a.shape, lambda i, b: (0,) * a.ndim)
    small = pl.BlockSpec((None, nseq * SUBLANES, d), lambda i, b: (i, 0, 0))
    return pl.pallas_call(
        functools.partial(_tail_kernel, tt=tt),
        grid=(g, r // tm),
        in_specs=[tok(d), tok(mix_a.shape[2]), tok(mix_b.shape[2]), small,
                  const(w_out0), const(norm_w1), const(sc_w_in), const(sc_conv_w), const(sc_w_out),
                  const(final_norm_w)],
        out_specs=(tok(d), small),
        out_shape=(jax.ShapeDtypeStruct((g, r, d), F32),
                   jax.ShapeDtypeStruct((g, nseq * SUBLANES, d), F32)),
        scratch_shapes=[pltpu.VMEM((nseq * SUBLANES, d), F32)],
        compiler_params=_cparams(("parallel", "arbitrary")), name="tail",
    )(x, mix_a, mix_b, prev, w_out0, norm_w1, sc_w_in, sc_conv_w, sc_w_out, final_norm_w)


def _pad_front_rows(buf, rows):
    return jnp.pad(buf, ((0, 0), (rows - buf.shape[1], 0), (0, 0)))


def _trunk(x, weights, sb_fn, gdn_state, gdn_buf, sc_buf, *, kv_transposed, tm_proj, gdn_tb, gdn_chunk,
           tail_tm, tail_tt):
    (norm_w, w_main, w_kvt, w_tail, conv_w, gparams, out_norm_w, w_out0, sc_w_in, sc_conv_w, sc_w_out,
     final_norm_w) = weights
    n, t, d = x.shape
    sbw = w_out0.shape[0] // 2
    heads = sbw // SB_HEAD_DIM
    gvw = GDN_HEADS * GDN_DK
    gqkvw = conv_w.shape[1]
    xp = x if kv_transposed else x.reshape(1, n * t, d)
    q, kb, vb, kf, vf, sg, gqkv, gg, tail = _proj0(
        xp, norm_w[0:1], w_main, w_kvt, w_tail, sbw=sbw, gqkvw=gqkvw, gvw=gvw, tm=tm_proj,
        kv_transposed=kv_transposed)
    r3 = lambda a: a.reshape(n, t, a.shape[-1])
    if kv_transposed:
        new_kv = lambda a: jnp.transpose(a.reshape(n, heads, SB_HEAD_DIM, t), (0, 3, 1, 2))[None]
        mix_a = sb_fn(q, kb, vb, sg)
    else:
        new_kv = lambda a: a.reshape(1, n, t, heads, SB_HEAD_DIM)
        mix_a = sb_fn(r3(q), r3(kb), r3(vb), r3(sg))
    new_k, new_v = new_kv(kf), new_kv(vf)
    gqkv3 = r3(gqkv)
    mix_b, new_state = _gdn(gqkv3, r3(tail), r3(gg), _pad_front_rows(gdn_buf, SUBLANES), gdn_state,
                            conv_w, gparams, out_norm_w, tb=gdn_tb, chunk=gdn_chunk)
    prev_sc = _pad_front_rows(sc_buf, SUBLANES)
    if tail_tt == tail_tm:
        groups = (n, t)
    else:
        groups = (1, n * t)
        prev_sc = prev_sc.reshape(1, n * SUBLANES, d)
    shp = lambda a: a.reshape(groups + (a.shape[-1],))
    y, last = _tail(shp(x), shp(mix_a), shp(mix_b), prev_sc, w_out0, norm_w[1:2], sc_w_in, sc_conv_w,
                    sc_w_out, final_norm_w, tm=tail_tm, tt=tail_tt)
    new_gbuf = gqkv3[:, t - (GDN_CONV - 1):, :][None]
    new_scbuf = last.reshape(n, SUBLANES, d)[:, SUBLANES - (SC_CONV - 1):, :][None]
    return y.reshape(n, t, d), new_k, new_v, new_state[None], new_gbuf, new_scbuf


def kernel(x_prompt, x_sample, cache_sb_k, cache_sb_v, state_gdn, state_gdn_conv, state_sconv, page_table,
           norm_w, ab_w_in, ab_sb_bias, ab_conv_w, ab_a_log, ab_dt_bias, ab_out_norm_w, ab_w_out,
           sc_w_in, sc_conv_w, sc_w_out, final_norm_w):
    assert ab_w_in.shape[0] == 1 and sc_w_in.shape[0] == 1, "one layer of each kind"
    nb, t, d = x_prompt.shape
    ns, ts, _ = x_sample.shape
    heads = GDN_HEADS
    main_cols = ab_w_in.shape[2] - 2 * heads
    sbw = ab_w_out.shape[1] // 2
    w_main = ab_w_in[0, :, :main_cols].astype(BF16)
    w_kvt = ab_w_in[0, :, sbw:3 * sbw].T.astype(BF16)
    w_tail = jnp.pad(ab_w_in[0, :, main_cols:], ((0, 0), (0, LANES - 2 * heads))).astype(BF16)
    gparams = jnp.zeros((SUBLANES, LANES), F32)
    gparams = gparams.at[0, heads:2 * heads].set(ab_a_log[0]).at[1, heads:2 * heads].set(ab_dt_bias[0])
    weights = (norm_w, w_main, w_kvt, w_tail, ab_conv_w[0], gparams, ab_out_norm_w[0:1],
               ab_w_out[0].astype(BF16), sc_w_in[0].astype(BF16), sc_conv_w[0], sc_w_out[0].astype(BF16),
               final_norm_w[None])
    bias = ab_sb_bias[0]

    zeros = lambda *s: jnp.zeros(s, F32)
    tm = min(256, t)
    y_p, k_p, v_p, s_p, gb_p, sc_p = _trunk(
        x_prompt, weights, functools.partial(_sb_prompt, bias, tq=min(256, t), gp=4),
        zeros(nb, heads, GDN_DK, GDN_DK), zeros(nb, GDN_CONV - 1, ab_conv_w.shape[2]),
        zeros(nb, SC_CONV - 1, d),
        kv_transposed=True, tm_proj=tm, gdn_tb=min(128, t), gdn_chunk=min(64, t), tail_tm=tm, tail_tt=tm)

    n_pool, page = cache_sb_k.shape[1], cache_sb_k.shape[2]
    pool_t = lambda c: jnp.transpose(c, (0, 1, 3, 4, 2)).reshape(n_pool, -1, page)
    kpool = pool_t(cache_sb_k)
    vpool = pool_t(cache_sb_v)
    sb_sample = lambda q, kb, vb, sg: _sb_paged(page_table, bias, q, kb, vb, sg, kpool, vpool,
                                                pp=min(8, page_table.shape[1]))
    y_s, k_s, v_s, s_s, gb_s, sc_s = _trunk(
        x_sample, weights, sb_sample, state_gdn[0], state_gdn_conv[0], state_sconv[0],
        kv_transposed=False, tm_proj=ns * ts, gdn_tb=ts, gdn_chunk=ts, tail_tm=ns * ts, tail_tt=ts)
    return (y_p, y_s, k_p, v_p, k_s, v_s, s_p, s_s, gb_p, gb_s, sc_p, sc_s)
```

```python
import functools
import math

import jax
import jax.numpy as jnp
from jax import lax
from jax.experimental import pallas as pl
from jax.experimental.pallas import tpu as pltpu

F32 = jnp.float32
BF16 = jnp.bfloat16
LOG2E = 1.4426950408889634
NORM_EPS = 1e-6
LANES = 128
SUBLANES = 8
VMEM_LIMIT_BYTES = 56 * 1024 * 1024

SB_HEAD_DIM = 64
GDN_HEADS = 4
GDN_DK = 128
GDN_CONV = 4
SC_CONV = 3


def _cparams(sem):
    return pltpu.CompilerParams(dimension_semantics=sem, vmem_limit_bytes=VMEM_LIMIT_BYTES)


def _silu(x):
    return x * (1.0 / (1.0 + jnp.exp(-x)))


def _rms(x, w):
    return x * lax.rsqrt(jnp.mean(x * x, axis=-1, keepdims=True) + NORM_EPS) * w


def _mm(a, b):
    return jnp.dot(a.astype(BF16), b.astype(BF16), preferred_element_type=F32)


def _mm_nt(a, b):
    return lax.dot_general(a.astype(BF16), b.astype(BF16), (((1,), (1,)), ((), ())),
                           preferred_element_type=F32)


def _split3(x):
    hi = x.astype(BF16)
    r = x - hi.astype(F32)
    mid = r.astype(BF16)
    lo = (r - mid.astype(F32)).astype(BF16)
    return hi, mid, lo


def _mm3(a, b):
    ah, am, _ = _split3(a)
    bh, bm, _ = _split3(b)
    d = functools.partial(jnp.dot, preferred_element_type=F32)
    return d(ah, bh) + (d(ah, bm) + d(am, bh))


def _mm_exact_lhs(a01, b):
    a = a01.astype(BF16)
    bh, bm, bl = _split3(b)
    d = functools.partial(jnp.dot, preferred_element_type=F32)
    return d(a, bh) + (d(a, bm) + d(a, bl))


def _softplus2(z):
    neg_abs = pltpu.bitcast(pltpu.bitcast(z, jnp.uint32) | jnp.uint32(0x80000000), F32)
    return jnp.maximum(z, 0.0) + jnp.log2(1.0 + jnp.exp2(neg_abs))


def _proj0_kernel(x_ref, nw_ref, w_ref, wkvt_ref, wt_ref,
                  q_ref, kb_ref, vb_ref, kf_ref, vf_ref, sg_ref, gqkv_ref, gg_ref, tail_ref,
                  *, sbw, gqkvw, gvw, qscale, kv_transposed):
    x = x_ref[...]
    xn = _rms(x, nw_ref[...]).astype(BF16)

    def proj(lo, width):
        return jnp.dot(xn, w_ref[:, lo:lo + width], preferred_element_type=F32)

    q_ref[...] = (proj(0, sbw) * qscale).astype(BF16)
    if kv_transposed:
        nt = lambda wt: lax.dot_general(wt, xn, (((1,), (1,)), ((), ())), preferred_element_type=F32)
        k = nt(wkvt_ref[:sbw, :])
        v = nt(wkvt_ref[sbw:, :])
    else:
        k = proj(sbw, sbw)
        v = proj(2 * sbw, sbw)
    kf_ref[...] = k
    kb_ref[...] = k.astype(BF16)
    vf_ref[...] = v
    vb_ref[...] = v.astype(BF16)
    sg_ref[...] = _silu(proj(3 * sbw, sbw)).astype(BF16)
    gqkv_ref[...] = proj(4 * sbw, gqkvw)
    gg_ref[...] = _silu(proj(4 * sbw + gqkvw, gvw)).astype(BF16)
    tail_ref[...] = jnp.dot(xn, wt_ref[...], preferred_element_type=F32)


def _proj0(x, norm_w, w_main, w_kvt, w_tail, *, sbw, gqkvw, gvw, tm, kv_transposed):
    g, rows, d = x.shape
    assert rows % tm == 0
    qscale = LOG2E * SB_HEAD_DIM ** -0.5
    kern = functools.partial(_proj0_kernel, sbw=sbw, gqkvw=gqkvw, gvw=gvw, qscale=qscale,
                             kv_transposed=kv_transposed)
    row = lambda w: pl.BlockSpec((None, tm, w), lambda b, i: (b, i, 0))
    full = lambda a: pl.BlockSpec(a.shape, lambda b, i: (0,) * a.ndim)
    rs = lambda w, dt: jax.ShapeDtypeStruct((g, rows, w), dt)
    if kv_transposed:
        kv_spec = pl.BlockSpec((None, sbw, tm), lambda b, i: (b, 0, i))
        kv_shape = lambda dt: jax.ShapeDtypeStruct((g, sbw, rows), dt)
    else:
        kv_spec = row(sbw)
        kv_shape = lambda dt: rs(sbw, dt)
    out_shapes = (
        rs(sbw, BF16),
        kv_shape(BF16),
        kv_shape(BF16),
        kv_shape(F32),
        kv_shape(F32),
        rs(sbw, BF16),
        rs(gqkvw, F32),
        rs(gvw, BF16),
        rs(LANES, F32),
    )
    out_specs = (row(sbw), kv_spec, kv_spec, kv_spec, kv_spec, row(sbw), row(gqkvw), row(gvw), row(LANES))
    return pl.pallas_call(
        kern, grid=(g, rows // tm),
        in_specs=[row(d), full(norm_w), full(w_main), full(w_kvt), full(w_tail)],
        out_specs=out_specs, out_shape=out_shapes,
        compiler_params=_cparams(("parallel", "parallel")), name="proj0",
    )(x, norm_w, w_main, w_kvt, w_tail)


def _neg_suffix(tk):
    j = lax.broadcasted_iota(jnp.int32, (tk, tk), 0)
    s = lax.broadcasted_iota(jnp.int32, (tk, tk), 1)
    return jnp.where(j >= s, -1.0, 0.0).astype(BF16)


def _sb_tile(z, r_rep, negu, mask):
    tk = z.shape[1]
    sp = _softplus2(z)
    if mask is not None:
        sp = jnp.where(mask, sp, 0.0)
    incl = jnp.dot(sp.astype(BF16), negu, preferred_element_type=F32)
    r_all = r_rep if tk == LANES else jnp.concatenate([r_rep] * (tk // LANES), axis=1)
    a = jnp.exp2(z + incl + r_all)
    if mask is not None:
        a = jnp.where(mask, a, 0.0)
    return a.astype(BF16), r_rep + jnp.broadcast_to(incl[:, 0:1], r_rep.shape)


BIAS_PIECES = 3


def _sb_prompt_kernel(bias_ref, q_ref, k_ref, v_ref, g_ref, negu_ref, o_ref, kaug_ref, acc_ref, r_ref,
                      *, tq, gp):
    pg = pl.program_id(1)
    i = pl.program_id(2)
    hd = SB_HEAD_DIM

    @pl.when(i == 0)
    def _():
        rown = lax.broadcasted_iota(jnp.int32, (LANES, k_ref.shape[1]), 0)
        ones_rows = jnp.where(rown < BIAS_PIECES, 1.0, 0.0).astype(BF16)
        for g in range(gp):
            kaug_ref[g, :LANES, :] = k_ref[g * LANES:(g + 1) * LANES, :]
            kaug_ref[g, LANES:, :] = ones_rows

    lane = lax.broadcasted_iota(jnp.int32, (tq, LANES), 1)
    zero = jnp.zeros((tq, LANES), BF16)

    def bias_lanes(b):
        pieces = _split3(jnp.full((tq, LANES), b * LOG2E, F32))
        out = jnp.zeros((tq, LANES), F32)
        for n, piece in enumerate(pieces):
            out = jnp.where(lane == n, piece.astype(F32), out)
        return out.astype(BF16)

    def stacked_q(g):
        q = q_ref[:, g * LANES:(g + 1) * LANES]
        h0 = 2 * (pg * gp + g)
        return jnp.concatenate(
            [jnp.concatenate([jnp.where(lane < hd, q, zero), bias_lanes(bias_ref[h0])], axis=1),
             jnp.concatenate([jnp.where(lane >= hd, q, zero), bias_lanes(bias_ref[h0 + 1])], axis=1)],
            axis=0)

    q2 = [stacked_q(g) for g in range(gp)]
    negu = negu_ref[...]

    def span(j):
        return pl.ds(pl.multiple_of(j * tq, tq), tq)

    def tile(g, j, r_rep, mask):
        z = jnp.dot(q2[g], kaug_ref[g, :, span(j)], preferred_element_type=F32)
        a, r = _sb_tile(z, r_rep, negu, mask)
        return _mm_nt(a, v_ref[g * LANES:(g + 1) * LANES, span(j)]), r

    row = lax.broadcasted_iota(jnp.int32, (2 * tq, tq), 0)
    col = lax.broadcasted_iota(jnp.int32, (2 * tq, tq), 1)
    diag_mask = col < jnp.where(row >= tq, row - tq, row)
    for g in range(gp):
        pv, r = tile(g, i, jnp.zeros((2 * tq, LANES), F32), diag_mask)
        acc_ref[g] = pv
        r_ref[g] = r

    def body(jj, carry):
        j = i - 1 - jj
        zs = [jnp.dot(q2[g], kaug_ref[g, :, span(j)], preferred_element_type=F32) for g in range(gp)]
        for g in range(gp):
            a, r = _sb_tile(zs[g], r_ref[g], negu, None)
            acc_ref[g] += _mm_nt(a, v_ref[g * LANES:(g + 1) * LANES, span(j)])
            r_ref[g] = r
        return carry

    lax.fori_loop(0, i, body, 0)
    lane_o = lax.broadcasted_iota(jnp.int32, (tq, 2 * hd), 1)
    for g in range(gp):
        acc = acc_ref[g]
        o = jnp.where(lane_o < hd, acc[:tq], acc[tq:])
        cols = slice(g * LANES, (g + 1) * LANES)
        o_ref[:, cols] = (o * g_ref[:, cols].astype(F32)).astype(BF16)


def _sb_prompt(bias, q, kt, vt, gate, *, tq, gp):
    n, t, w = q.shape
    gw = gp * LANES
    assert w % gw == 0 and t % tq == 0 and tq % LANES == 0
    negu = _neg_suffix(tq)
    blk = pl.BlockSpec((None, tq, gw), lambda b, p, i: (b, i, p))
    seq = pl.BlockSpec((None, gw, t), lambda b, p, i: (b, p, 0))
    return pl.pallas_call(
        functools.partial(_sb_prompt_kernel, tq=tq, gp=gp),
        grid=(n, w // gw, t // tq),
        in_specs=[pl.BlockSpec(memory_space=pltpu.SMEM), blk, seq, seq, blk,
                  pl.BlockSpec(negu.shape, lambda b, p, i: (0, 0))],
        out_specs=blk,
        out_shape=jax.ShapeDtypeStruct((n, t, w), BF16),
        scratch_shapes=[pltpu.VMEM((gp, 2 * LANES, t), BF16), pltpu.VMEM((gp, 2 * tq, LANES), F32),
                        pltpu.VMEM((gp, 2 * tq, LANES), F32)],
        compiler_params=_cparams(("parallel", "parallel", "arbitrary")), name="sb_prompt",
    )(bias, q, kt, vt, gate, negu)


def _sb_paged_kernel(pt_ref, qbd_ref, kn_ref, vn_ref, g_ref, bias_ref, negu_ref, *rest, pp, heads, tnew):
    k_refs = rest[:pp]
    v_refs = rest[pp:2 * pp]
    o_ref, acc_ref, r_ref = rest[2 * pp:]
    g = pl.program_id(1)
    qbd = qbd_ref[...]
    bias = bias_ref[...]
    negu = negu_ref[...]
    rows, page = bias.shape

    @pl.when(g == 0)
    def _():
        z = _mm_nt(qbd, kn_ref[...]) + bias
        row = lax.broadcasted_iota(jnp.int32, (rows, page), 0)
        col = lax.broadcasted_iota(jnp.int32, (rows, page), 1)
        mask = col < (row % tnew)
        a, r = _sb_tile(z, jnp.zeros((rows, page), F32), negu[:page, :page], mask)
        acc_ref[...] = _mm_nt(vn_ref[...], a)
        r_ref[...] = r

    r = r_ref[...]
    acc = acc_ref[...]
    bias2 = jnp.concatenate([bias, bias], axis=1)
    pair = lambda refs, w: jnp.concatenate([refs[2 * w][...].astype(BF16), refs[2 * w + 1][...].astype(BF16)],
                                           axis=1)
    for w in reversed(range(pp // 2)):
        z = jnp.dot(qbd, pair(k_refs, w), preferred_element_type=F32) + bias2
        a, r = _sb_tile(z, r, negu, None)
        acc = acc + _mm_nt(pair(v_refs, w), a)
    acc_ref[...] = acc
    r_ref[...] = r

    @pl.when(g == pl.num_programs(1) - 1)
    def _():
        width = acc_ref.shape[0]
        eye = (lax.broadcasted_iota(jnp.int32, (rows, rows), 0)
               == lax.broadcasted_iota(jnp.int32, (rows, rows), 1)).astype(F32).astype(BF16)
        acc_t = sum(_mm_nt(eye, piece) for piece in _split3(acc_ref[...]))
        lane_h = lax.broadcasted_iota(jnp.int32, (tnew, width), 1) // SB_HEAD_DIM
        o = jnp.zeros((tnew, width), F32)
        for h in range(heads):
            o = o + jnp.where(lane_h == h, acc_t[h * tnew:(h + 1) * tnew], 0.0)
        o_ref[...] = (o * g_ref[...].astype(F32)).astype(BF16)


def _sb_paged(page_table, bias, q, k_new, v_new, gate, kpool, vpool, *, pp):
    n, tnew, w = q.shape
    heads = w // SB_HEAD_DIM
    npages = page_table.shape[1]
    page = kpool.shape[2]
    assert npages % pp == 0 and pp % 2 == 0 and tnew <= page and tnew % SUBLANES == 0
    rows = heads * tnew
    head_of_lane = jnp.arange(w) // SB_HEAD_DIM
    sel = (head_of_lane[None, :] == jnp.arange(heads)[:, None])
    qbd = jnp.where(sel[None, :, None, :], q[:, None, :, :], 0).reshape(n, rows, w)
    pad = ((0, 0), (0, page - tnew), (0, 0))
    kn = jnp.pad(k_new, pad)
    vn = jnp.swapaxes(jnp.pad(v_new, pad), 1, 2)
    bias_rep = jnp.broadcast_to(jnp.repeat(bias.astype(F32) * LOG2E, tnew)[:, None], (rows, page))
    negu = _neg_suffix(2 * page)

    def page_map(b, g, pt, *, r):
        return (pt[b, npages - (g + 1) * pp + r], 0, 0)

    per_seq = lambda shape: pl.BlockSpec((None,) + shape, lambda b, g, pt: (b, 0, 0))
    const = lambda a: pl.BlockSpec(a.shape, lambda b, g, pt: (0,) * a.ndim)
    page_specs = [pl.BlockSpec((None, w, page), functools.partial(page_map, r=r)) for r in range(pp)]
    grid_spec = pltpu.PrefetchScalarGridSpec(
        num_scalar_prefetch=1, grid=(n, npages // pp),
        in_specs=[per_seq((rows, w)), per_seq((page, w)), per_seq((w, page)), per_seq((tnew, w)),
                  const(bias_rep), const(negu)] + page_specs + page_specs,
        out_specs=per_seq((tnew, w)),
        scratch_shapes=[pltpu.VMEM((w, rows), F32), pltpu.VMEM((rows, page), F32)],
    )
    return pl.pallas_call(
        functools.partial(_sb_paged_kernel, pp=pp, heads=heads, tnew=tnew),
        grid_spec=grid_spec,
        out_shape=jax.ShapeDtypeStruct((n, tnew, w), BF16),
        compiler_params=_cparams(("parallel", "arbitrary")), name="sb_paged",
    )(page_table, qbd, kn, vn, gate, bias_rep, negu, *([kpool] * pp), *([vpool] * pp))


def _causal_conv_rows(x, prev8, w, width):
    rows = x.shape[0]
    xx = jnp.concatenate([prev8, x], axis=0)
    y = x * w[width - 1:width, :]
    for s in range(1, width):
        y = y + pltpu.roll(xx, s, 0)[SUBLANES:SUBLANES + rows] * w[width - 1 - s:width - s, :]
    return y


def _transpose_exact(x):
    r = lax.broadcasted_iota(jnp.int32, (LANES, LANES), 0)
    c = lax.broadcasted_iota(jnp.int32, (LANES, LANES), 1)
    eye = jnp.where(r == c, 1.0, 0.0).astype(BF16)
    hi, mid, lo = _split3(x)
    return _mm_nt(eye, hi) + (_mm_nt(eye, mid) + _mm_nt(eye, lo))


def _gdn_kernel(x_ref, tail_ref, gate_ref, prev_ref, s0_ref, cw_ref, gp_ref, onw_ref,
                o_ref, sout_ref, s_ref, carry_ref, *, chunk):
    b = pl.program_id(1)
    heads, dk = GDN_HEADS, GDN_DK
    tb = x_ref.shape[0]
    nc = tb // chunk
    qkw = heads * dk

    @pl.when(b == 0)
    def _():
        s_ref[...] = s0_ref[...]
        carry_ref[...] = prev_ref[...]

    x = x_ref[...]
    y = _silu(_causal_conv_rows(x, carry_ref[...], cw_ref[...], GDN_CONV))
    carry_ref[...] = x[tb - SUBLANES:, :]

    tl = tail_ref[...]
    gp = gp_ref[...]
    beta_all = 1.0 / (1.0 + jnp.exp(-tl))
    xg = tl + gp[1:2, :]
    g_all = -jnp.exp(gp[0:1, :]) * (jnp.maximum(xg, 0.0) + jnp.log(1.0 + jnp.exp(-jnp.abs(xg))))

    ri = lax.broadcasted_iota(jnp.int32, (tb, tb), 0)
    ci = lax.broadcasted_iota(jnp.int32, (tb, tb), 1)
    same = (ri // chunk) == (ci // chunk)
    incl = same & (ri >= ci)
    strict = same & (ri > ci)
    eye = jnp.where(ri == ci, 1.0, 0.0)
    sums = _mm_exact_lhs(jnp.concatenate([jnp.where(incl, 1.0, 0.0), jnp.where(same, 1.0, 0.0)], axis=0), g_all)
    gc_all, gl_all = sums[:tb], sums[tb:]
    gc_t = _transpose_exact(gc_all)
    onw = onw_ref[...]

    hs = range(heads)
    each = lambda f: [f(h) for h in hs]
    q = each(lambda h: y[:, h * dk:(h + 1) * dk])
    k = each(lambda h: y[:, qkw + h * dk:qkw + (h + 1) * dk])
    v = each(lambda h: y[:, 2 * qkw + h * dk:2 * qkw + (h + 1) * dk])
    q = each(lambda h: q[h] * (lax.rsqrt(jnp.sum(q[h] * q[h], -1, keepdims=True) + NORM_EPS) * dk ** -0.5))
    k = each(lambda h: k[h] * lax.rsqrt(jnp.sum(k[h] * k[h], -1, keepdims=True) + NORM_EPS))
    beta = each(lambda h: beta_all[:, h:h + 1])
    gc = each(lambda h: gc_all[:, heads + h:heads + h + 1])
    gl = each(lambda h: gl_all[:, heads + h:heads + h + 1])
    decay = each(lambda h: jnp.exp(jnp.where(
        incl, gc[h] - jnp.broadcast_to(gc_t[heads + h:heads + h + 1, :], (tb, tb)), 0.0)))
    kb = each(lambda h: k[h] * beta[h])
    kk = each(lambda h: _mm_nt(kb[h], k[h]))
    qk = each(lambda h: _mm_nt(q[h], k[h]))
    m = each(lambda h: jnp.where(strict, kk[h] * decay[h], 0.0))
    qk = each(lambda h: jnp.where(incl, qk[h] * decay[h], 0.0))
    egc = each(lambda h: jnp.exp(gc[h]))
    rhs = each(lambda h: jnp.concatenate([v[h] * beta[h], kb[h] * egc[h]], axis=1))
    inv = each(lambda h: eye - m[h])
    p = m
    for _ in range(max(0, int(math.log2(chunk)) - 1)):
        p = each(lambda h: _mm(p[h], p[h]))
        inv = each(lambda h: inv[h] + _mm(inv[h], p[h]))
    uw = each(lambda h: _mm(inv[h], rhs[h]))
    res = each(lambda h: rhs[h] - uw[h] - _mm3(m[h], uw[h]))
    uw = each(lambda h: uw[h] + _mm(inv[h], res[h]))
    qe = each(lambda h: q[h] * egc[h])
    k_dec = each(lambda h: k[h] * jnp.exp(gl[h] - gc[h]))
    s_dec = each(lambda h: jnp.exp(gl[h]))

    s = each(lambda h: s_ref[h])
    v_parts = [[] for _ in hs]
    o_parts = [[] for _ in hs]
    for c in range(nc):
        rows = slice(c * chunk, (c + 1) * chunk)
        v_new = each(lambda h: uw[h][rows, :dk] - _mm(uw[h][rows, dk:], s[h]))
        for h in hs:
            v_parts[h].append(v_new[h])
        pad = [jnp.zeros(((nc - 1 - c) * chunk, dk), F32)] * (c < nc - 1)
        o_c = each(lambda h: _mm(qe[h][rows], s[h]) + _mm(qk[h][rows], jnp.concatenate(v_parts[h] + pad, axis=0)))
        for h in hs:
            o_parts[h].append(o_c[h])
        s = each(lambda h: s[h] * s_dec[h][c * chunk:c * chunk + 1, :] + lax.dot_general(
            k_dec[h][rows].astype(BF16), v_new[h].astype(BF16), (((0,), (0,)), ((), ())),
            preferred_element_type=F32))
    for h in hs:
        s_ref[h] = s[h]
        o = jnp.concatenate(o_parts[h], axis=0)
        on = o * lax.rsqrt(jnp.mean(o * o, -1, keepdims=True) + NORM_EPS) * onw
        gate = gate_ref[:, h * dk:(h + 1) * dk].astype(F32)
        o_ref[:, h * dk:(h + 1) * dk] = (on * gate).astype(BF16)

    @pl.when(b == pl.num_programs(1) - 1)
    def _():
        sout_ref[...] = s_ref[...]


def _gdn(gqkv, tail, gate, prev8, s0, conv_w, gparams, out_norm_w, *, tb, chunk):
    n, t, w = gqkv.shape
    heads, dk = GDN_HEADS, GDN_DK
    assert t % tb == 0 and tb % chunk == 0 and tb % SUBLANES == 0
    tok = lambda width: pl.BlockSpec((None, tb, width), lambda i, b: (i, b, 0))
    const = lambda a: pl.BlockSpec(a.shape, lambda i, b: (0,) * a.ndim)
    state = pl.BlockSpec((None, heads, dk, dk), lambda i, b: (i, 0, 0, 0))
    return pl.pallas_call(
        functools.partial(_gdn_kernel, chunk=chunk),
        grid=(n, t // tb),
        in_specs=[tok(w), tok(LANES), tok(heads * dk),
                  pl.BlockSpec((None, SUBLANES, w), lambda i, b: (i, 0, 0)), state,
                  const(conv_w), const(gparams), const(out_norm_w)],
        out_specs=(tok(heads * dk), state),
        out_shape=(jax.ShapeDtypeStruct((n, t, heads * dk), BF16),
                   jax.ShapeDtypeStruct((n, heads, dk, dk), F32)),
        scratch_shapes=[pltpu.VMEM((heads, dk, dk), F32), pltpu.VMEM((SUBLANES, w), F32)],
        compiler_params=_cparams(("parallel", "arbitrary")), name="gdn",
    )(gqkv, tail, gate, prev8, s0, conv_w, gparams, out_norm_w)


def _tail_kernel(x_ref, ma_ref, mb_ref, prev_ref, wo_ref, nw_ref, win_ref, cw_ref, wout_ref, fnw_ref,
                 y_ref, last_ref, carry_ref, *, tt):
    b = pl.program_id(1)
    rows, d = x_ref.shape
    sa = ma_ref.shape[1]

    @pl.when(b == 0)
    def _():
        carry_ref[...] = prev_ref[...]

    h1 = x_ref[...] + (jnp.dot(ma_ref[...], wo_ref[:sa, :], preferred_element_type=F32)
                       + jnp.dot(mb_ref[...], wo_ref[sa:, :], preferred_element_type=F32))
    xn = _rms(h1, nw_ref[...]).astype(BF16)
    proj = lambda i: jnp.dot(xn, win_ref[:, i * d:(i + 1) * d], preferred_element_type=F32)
    cu = proj(2) * proj(0)
    cw = cw_ref[...]
    if tt == rows:
        yc = _causal_conv_rows(cu, carry_ref[...], cw, SC_CONV)
        carry_ref[...] = cu[rows - SUBLANES:, :]
        last_ref[...] = cu[rows - SUBLANES:, :]
    else:
        assert tt == SUBLANES
        prev = carry_ref[...]
        tok = lax.broadcasted_iota(jnp.int32, (rows, d), 0) % tt
        yc = cu * cw[SC_CONV - 1:SC_CONV, :]
        for s in range(1, SC_CONV):
            shifted = jnp.where(tok >= s, pltpu.roll(cu, s, 0), pltpu.roll(prev, (s - tt) % rows, 0))
            yc = yc + shifted * cw[SC_CONV - 1 - s:SC_CONV - s, :]
        last_ref[...] = cu
    m = (proj(1) * yc * _silu(proj(3))).astype(BF16)
    h2 = h1 + jnp.dot(m, wout_ref[...], preferred_element_type=F32)
    y_ref[...] = _rms(h2, fnw_ref[...])


def _tail(x, mix_a, mix_b, prev, w_out0, norm_w1, sc_w_in, sc_conv_w, sc_w_out, final_norm_w, *, tm, tt):
    g, r, d = x.shape
    assert r % tm == 0 and tm % tt == 0 and (tt == tm or (tt == SUBLANES and r == tm))
    nseq = tm // tt
    tok = lambda width: pl.BlockSpec((None, tm, width), lambda i, b: (i, b, 0))
    const = lambda a: pl.BlockSpec(a.shape, lambda i, b: (0,) * a.ndim)
    small = pl.BlockSpec((None, nseq * SUBLANES, d), lambda i, b: (i, 0, 0))
    return pl.pallas_call(
        functools.partial(_tail_kernel, tt=tt),
        grid=(g, r // tm),
        in_specs=[tok(d), tok(mix_a.shape[2]), tok(mix_b.shape[2]), small,
                  const(w_out0), const(norm_w1), const(sc_w_in), const(sc_conv_w), const(sc_w_out),
                  const(final_norm_w)],
        out_specs=(tok(d), small),
        out_shape=(jax.ShapeDtypeStruct((g, r, d), F32),
                   jax.ShapeDtypeStruct((g, nseq * SUBLANES, d), F32)),
        scratch_shapes=[pltpu.VMEM((nseq * SUBLANES, d), F32)],
        compiler_params=_cparams(("parallel", "arbitrary")), name="tail",
    )(x, mix_a, mix_b, prev, w_out0, norm_w1, sc_w_in, sc_conv_w, sc_w_out, final_norm_w)


def _pad_front_rows(buf, rows):
    return jnp.pad(buf, ((0, 0), (rows - buf.shape[1], 0), (0, 0)))


def _trunk(x, weights, sb_fn, gdn_state, gdn_buf, sc_buf, *, kv_transposed, tm_proj, gdn_tb, gdn_chunk,
           tail_tm, tail_tt):
    (norm_w, w_main, w_kvt, w_tail, conv_w, gparams, out_norm_w, w_out0, sc_w_in, sc_conv_w, sc_w_out,
     final_norm_w) = weights
    n, t, d = x.shape
    sbw = w_out0.shape[0] // 2
    heads = sbw // SB_HEAD_DIM
    gvw = GDN_HEADS * GDN_DK
    gqkvw = conv_w.shape[1]
    xp = x if kv_transposed else x.reshape(1, n * t, d)
    q, kb, vb, kf, vf, sg, gqkv, gg, tail = _proj0(
        xp, norm_w[0:1], w_main, w_kvt, w_tail, sbw=sbw, gqkvw=gqkvw, gvw=gvw, tm=tm_proj,
        kv_transposed=kv_transposed)
    r3 = lambda a: a.reshape(n, t, a.shape[-1])
    if kv_transposed:
        new_kv = lambda a: jnp.transpose(a.reshape(n, heads, SB_HEAD_DIM, t), (0, 3, 1, 2))[None]
        mix_a = sb_fn(q, kb, vb, sg)
    else:
        new_kv = lambda a: a.reshape(1, n, t, heads, SB_HEAD_DIM)
        mix_a = sb_fn(r3(q), r3(kb), r3(vb), r3(sg))
    new_k, new_v = new_kv(kf), new_kv(vf)
    gqkv3 = r3(gqkv)
    mix_b, new_state = _gdn(gqkv3, r3(tail), r3(gg), _pad_front_rows(gdn_buf, SUBLANES), gdn_state,
                            conv_w, gparams, out_norm_w, tb=gdn_tb, chunk=gdn_chunk)
    prev_sc = _pad_front_rows(sc_buf, SUBLANES)
    if tail_tt == tail_tm:
        groups = (n, t)
    else:
        groups = (1, n * t)
        prev_sc = prev_sc.reshape(1, n * SUBLANES, d)
    shp = lambda a: a.reshape(groups + (a.shape[-1],))
    y, last = _tail(shp(x), shp(mix_a), shp(mix_b), prev_sc, w_out0, norm_w[1:2], sc_w_in, sc_conv_w,
                    sc_w_out, final_norm_w, tm=tail_tm, tt=tail_tt)
    new_gbuf = gqkv3[:, t - (GDN_CONV - 1):, :][None]
    new_scbuf = last.reshape(n, SUBLANES, d)[:, SUBLANES - (SC_CONV - 1):, :][None]
    return y.reshape(n, t, d), new_k, new_v, new_state[None], new_gbuf, new_scbuf


def kernel(x_prompt, x_sample, cache_sb_k, cache_sb_v, state_gdn, state_gdn_conv, state_sconv, page_table,
           norm_w, ab_w_in, ab_sb_bias, ab_conv_w, ab_a_log, ab_dt_bias, ab_out_norm_w, ab_w_out,
           sc_w_in, sc_conv_w, sc_w_out, final_norm_w):
    assert ab_w_in.shape[0] == 1 and sc_w_in.shape[0] == 1, "one layer of each kind"
    nb, t, d = x_prompt.shape
    ns, ts, _ = x_sample.shape
    heads = GDN_HEADS
    main_cols = ab_w_in.shape[2] - 2 * heads
    sbw = ab_w_out.shape[1] // 2
    w_main = ab_w_in[0, :, :main_cols].astype(BF16)
    w_kvt = ab_w_in[0, :, sbw:3 * sbw].T.astype(BF16)
    w_tail = jnp.pad(ab_w_in[0, :, main_cols:], ((0, 0), (0, LANES - 2 * heads))).astype(BF16)
    gparams = jnp.zeros((SUBLANES, LANES), F32)
    gparams = gparams.at[0, heads:2 * heads].set(ab_a_log[0]).at[1, heads:2 * heads].set(ab_dt_bias[0])
    weights = (norm_w, w_main, w_kvt, w_tail, ab_conv_w[0], gparams, ab_out_norm_w[0:1],
               ab_w_out[0].astype(BF16), sc_w_in[0].astype(BF16), sc_conv_w[0], sc_w_out[0].astype(BF16),
               final_norm_w[None])
    bias = ab_sb_bias[0]

    zeros = lambda *s: jnp.zeros(s, F32)
    tm = min(256, t)
    y_p, k_p, v_p, s_p, gb_p, sc_p = _trunk(
        x_prompt, weights, functools.partial(_sb_prompt, bias, tq=min(256, t), gp=4),
        zeros(nb, heads, GDN_DK, GDN_DK), zeros(nb, GDN_CONV - 1, ab_conv_w.shape[2]),
        zeros(nb, SC_CONV - 1, d),
        kv_transposed=True, tm_proj=tm, gdn_tb=min(256, t), gdn_chunk=min(64, t), tail_tm=tm, tail_tt=tm)

    n_pool, page = cache_sb_k.shape[1], cache_sb_k.shape[2]
    pool_t = lambda c: jnp.transpose(c, (0, 1, 3, 4, 2)).reshape(n_pool, -1, page)
    kpool = pool_t(cache_sb_k)
    vpool = pool_t(cache_sb_v)
    sb_sample = lambda q, kb, vb, sg: _sb_paged(page_table, bias, q, kb, vb, sg, kpool, vpool,
                                                pp=min(8, page_table.shape[1]))
    y_s, k_s, v_s, s_s, gb_s, sc_s = _trunk(
        x_sample, weights, sb_sample, state_gdn[0], state_gdn_conv[0], state_sconv[0],
        kv_transposed=False, tm_proj=ns * ts, gdn_tb=ts, gdn_chunk=ts, tail_tm=ns * ts, tail_tt=ts)
    return (y_p, y_s, k_p, v_p, k_s, v_s, s_p, s_s, gb_p, gb_s, sc_p, sc_s)
```

```python
import functools
import math

import jax
import jax.numpy as jnp
from jax import lax
from jax.experimental import pallas as pl
from jax.experimental.pallas import tpu as pltpu

F32 = jnp.float32
BF16 = jnp.bfloat16
LOG2E = 1.4426950408889634
NORM_EPS = 1e-6
LANES = 128
SUBLANES = 8
VMEM_LIMIT_BYTES = 56 * 1024 * 1024

SB_HEAD_DIM = 64
GDN_HEADS = 4
GDN_DK = 128
GDN_CONV = 4
SC_CONV = 3


def _cparams(sem):
    return pltpu.CompilerParams(dimension_semantics=sem, vmem_limit_bytes=VMEM_LIMIT_BYTES)


def _silu(x):
    return x * (1.0 / (1.0 + jnp.exp(-x)))


def _rms(x, w):
    return x * lax.rsqrt(jnp.mean(x * x, axis=-1, keepdims=True) + NORM_EPS) * w


def _mm(a, b):
    return jnp.dot(a.astype(BF16), b.astype(BF16), preferred_element_type=F32)


def _mm_nt(a, b):
    return lax.dot_general(a.astype(BF16), b.astype(BF16), (((1,), (1,)), ((), ())),
                           preferred_element_type=F32)


def _split3(x):
    hi = x.astype(BF16)
    r = x - hi.astype(F32)
    mid = r.astype(BF16)
    lo = (r - mid.astype(F32)).astype(BF16)
    return hi, mid, lo


def _mm3(a, b):
    ah, am, _ = _split3(a)
    bh, bm, _ = _split3(b)
    d = functools.partial(jnp.dot, preferred_element_type=F32)
    return d(ah, bh) + (d(ah, bm) + d(am, bh))


def _mm_exact_lhs(a01, b):
    a = a01.astype(BF16)
    bh, bm, bl = _split3(b)
    d = functools.partial(jnp.dot, preferred_element_type=F32)
    return d(a, bh) + (d(a, bm) + d(a, bl))


def _softplus2(z):
    neg_abs = pltpu.bitcast(pltpu.bitcast(z, jnp.uint32) | jnp.uint32(0x80000000), F32)
    return jnp.maximum(z, 0.0) + jnp.log2(1.0 + jnp.exp2(neg_abs))


def _proj0_kernel(x_ref, nw_ref, w_ref, wkvt_ref, wt_ref,
                  q_ref, kb_ref, vb_ref, kf_ref, vf_ref, sg_ref, gqkv_ref, gg_ref, tail_ref,
                  *, sbw, gqkvw, gvw, qscale, kv_transposed):
    x = x_ref[...]
    xn = _rms(x, nw_ref[...]).astype(BF16)

    def proj(lo, width):
        return jnp.dot(xn, w_ref[:, lo:lo + width], preferred_element_type=F32)

    q_ref[...] = (proj(0, sbw) * qscale).astype(BF16)
    if kv_transposed:
        nt = lambda wt: lax.dot_general(wt, xn, (((1,), (1,)), ((), ())), preferred_element_type=F32)
        k = nt(wkvt_ref[:sbw, :])
        v = nt(wkvt_ref[sbw:, :])
    else:
        k = proj(sbw, sbw)
        v = proj(2 * sbw, sbw)
    kf_ref[...] = k
    kb_ref[...] = k.astype(BF16)
    vf_ref[...] = v
    vb_ref[...] = v.astype(BF16)
    sg_ref[...] = _silu(proj(3 * sbw, sbw)).astype(BF16)
    gqkv_ref[...] = proj(4 * sbw, gqkvw)
    gg_ref[...] = _silu(proj(4 * sbw + gqkvw, gvw)).astype(BF16)
    tail_ref[...] = jnp.dot(xn, wt_ref[...], preferred_element_type=F32)


def _proj0(x, norm_w, w_main, w_kvt, w_tail, *, sbw, gqkvw, gvw, tm, kv_transposed):
    g, rows, d = x.shape
    assert rows % tm == 0
    qscale = LOG2E * SB_HEAD_DIM ** -0.5
    kern = functools.partial(_proj0_kernel, sbw=sbw, gqkvw=gqkvw, gvw=gvw, qscale=qscale,
                             kv_transposed=kv_transposed)
    row = lambda w: pl.BlockSpec((None, tm, w), lambda b, i: (b, i, 0))
    full = lambda a: pl.BlockSpec(a.shape, lambda b, i: (0,) * a.ndim, pipeline_mode=pl.Buffered(1))
    rs = lambda w, dt: jax.ShapeDtypeStruct((g, rows, w), dt)
    if kv_transposed:
        kv_spec = pl.BlockSpec((None, sbw, tm), lambda b, i: (b, 0, i))
        kv_shape = lambda dt: jax.ShapeDtypeStruct((g, sbw, rows), dt)
    else:
        kv_spec = row(sbw)
        kv_shape = lambda dt: rs(sbw, dt)
    out_shapes = (
        rs(sbw, BF16),
        kv_shape(BF16),
        kv_shape(BF16),
        kv_shape(F32),
        kv_shape(F32),
        rs(sbw, BF16),
        rs(gqkvw, F32),
        rs(gvw, BF16),
        rs(LANES, F32),
    )
    out_specs = (row(sbw), kv_spec, kv_spec, kv_spec, kv_spec, row(sbw), row(gqkvw), row(gvw), row(LANES))
    return pl.pallas_call(
        kern, grid=(g, rows // tm),
        in_specs=[row(d), full(norm_w), full(w_main), full(w_kvt), full(w_tail)],
        out_specs=out_specs, out_shape=out_shapes,
        compiler_params=_cparams(("parallel", "parallel")), name="proj0",
    )(x, norm_w, w_main, w_kvt, w_tail)


def _neg_suffix(tk):
    j = lax.broadcasted_iota(jnp.int32, (tk, tk), 0)
    s = lax.broadcasted_iota(jnp.int32, (tk, tk), 1)
    return jnp.where(j >= s, -1.0, 0.0).astype(BF16)


def _sb_tile(z, r_rep, negu, mask):
    tk = z.shape[1]
    sp = _softplus2(z)
    if mask is not None:
        sp = jnp.where(mask, sp, 0.0)
    incl = jnp.dot(sp.astype(BF16), negu, preferred_element_type=F32)
    r_all = r_rep if tk == LANES else jnp.concatenate([r_rep] * (tk // LANES), axis=1)
    a = jnp.exp2(z + incl + r_all)
    if mask is not None:
        a = jnp.where(mask, a, 0.0)
    return a.astype(BF16), r_rep + jnp.broadcast_to(incl[:, 0:1], r_rep.shape)


BIAS_PIECES = 3


def _sb_prompt_kernel(bias_ref, q_ref, k_ref, v_ref, g_ref, negu_ref, o_ref, kaug_ref, acc_ref, r_ref,
                      *, tq, gp):
    pg = pl.program_id(1)
    i = pl.program_id(2)
    hd = SB_HEAD_DIM

    @pl.when(i == 0)
    def _():
        rown = lax.broadcasted_iota(jnp.int32, (LANES, k_ref.shape[1]), 0)
        ones_rows = jnp.where(rown < BIAS_PIECES, 1.0, 0.0).astype(BF16)
        for g in range(gp):
            kaug_ref[g, :LANES, :] = k_ref[g * LANES:(g + 1) * LANES, :]
            kaug_ref[g, LANES:, :] = ones_rows

    lane = lax.broadcasted_iota(jnp.int32, (tq, LANES), 1)
    zero = jnp.zeros((tq, LANES), BF16)

    def bias_lanes(b):
        pieces = _split3(jnp.full((tq, LANES), b * LOG2E, F32))
        out = jnp.zeros((tq, LANES), F32)
        for n, piece in enumerate(pieces):
            out = jnp.where(lane == n, piece.astype(F32), out)
        return out.astype(BF16)

    def stacked_q(g):
        q = q_ref[:, g * LANES:(g + 1) * LANES]
        h0 = 2 * (pg * gp + g)
        return jnp.concatenate(
            [jnp.concatenate([jnp.where(lane < hd, q, zero), bias_lanes(bias_ref[h0])], axis=1),
             jnp.concatenate([jnp.where(lane >= hd, q, zero), bias_lanes(bias_ref[h0 + 1])], axis=1)],
            axis=0)

    q2 = [stacked_q(g) for g in range(gp)]
    negu = negu_ref[...]

    def span(j):
        return pl.ds(pl.multiple_of(j * tq, tq), tq)

    def tile(g, j, r_rep, mask):
        z = jnp.dot(q2[g], kaug_ref[g, :, span(j)], preferred_element_type=F32)
        a, r = _sb_tile(z, r_rep, negu, mask)
        return _mm_nt(a, v_ref[g * LANES:(g + 1) * LANES, span(j)]), r

    row = lax.broadcasted_iota(jnp.int32, (2 * tq, tq), 0)
    col = lax.broadcasted_iota(jnp.int32, (2 * tq, tq), 1)
    diag_mask = col < jnp.where(row >= tq, row - tq, row)
    for g in range(gp):
        pv, r = tile(g, i, jnp.zeros((2 * tq, LANES), F32), diag_mask)
        acc_ref[g] = pv
        r_ref[g] = r

    def body(jj, carry):
        j = i - 1 - jj
        zs = [jnp.dot(q2[g], kaug_ref[g, :, span(j)], preferred_element_type=F32) for g in range(gp)]
        for g in range(gp):
            a, r = _sb_tile(zs[g], r_ref[g], negu, None)
            acc_ref[g] += _mm_nt(a, v_ref[g * LANES:(g + 1) * LANES, span(j)])
            r_ref[g] = r
        return carry

    lax.fori_loop(0, i, body, 0)
    lane_o = lax.broadcasted_iota(jnp.int32, (tq, 2 * hd), 1)
    for g in range(gp):
        acc = acc_ref[g]
        o = jnp.where(lane_o < hd, acc[:tq], acc[tq:])
        cols = slice(g * LANES, (g + 1) * LANES)
        o_ref[:, cols] = (o * g_ref[:, cols].astype(F32)).astype(BF16)


def _sb_prompt(bias, q, kt, vt, gate, *, tq, gp):
    n, t, w = q.shape
    gw = gp * LANES
    assert w % gw == 0 and t % tq == 0 and tq % LANES == 0
    negu = _neg_suffix(tq)
    blk = pl.BlockSpec((None, tq, gw), lambda b, p, i: (b, i, p))
    seq = pl.BlockSpec((None, gw, t), lambda b, p, i: (b, p, 0))
    return pl.pallas_call(
        functools.partial(_sb_prompt_kernel, tq=tq, gp=gp),
        grid=(n, w // gw, t // tq),
        in_specs=[pl.BlockSpec(memory_space=pltpu.SMEM), blk, seq, seq, blk,
                  pl.BlockSpec(negu.shape, lambda b, p, i: (0, 0))],
        out_specs=blk,
        out_shape=jax.ShapeDtypeStruct((n, t, w), BF16),
        scratch_shapes=[pltpu.VMEM((gp, 2 * LANES, t), BF16), pltpu.VMEM((gp, 2 * tq, LANES), F32),
                        pltpu.VMEM((gp, 2 * tq, LANES), F32)],
        compiler_params=_cparams(("parallel", "parallel", "arbitrary")), name="sb_prompt",
    )(bias, q, kt, vt, gate, negu)


def _sb_paged_kernel(pt_ref, qbd_ref, kn_ref, vn_ref, g_ref, bias_ref, negu_ref, *rest, pp, heads, tnew):
    k_refs = rest[:pp]
    v_refs = rest[pp:2 * pp]
    o_ref, acc_ref, r_ref = rest[2 * pp:]
    g = pl.program_id(1)
    qbd = qbd_ref[...]
    bias = bias_ref[...]
    negu = negu_ref[...]
    rows, page = bias.shape

    @pl.when(g == 0)
    def _():
        z = _mm_nt(qbd, kn_ref[...]) + bias
        row = lax.broadcasted_iota(jnp.int32, (rows, page), 0)
        col = lax.broadcasted_iota(jnp.int32, (rows, page), 1)
        mask = col < (row % tnew)
        a, r = _sb_tile(z, jnp.zeros((rows, page), F32), negu[:page, :page], mask)
        acc_ref[...] = _mm_nt(vn_ref[...], a)
        r_ref[...] = r

    bias2 = jnp.concatenate([bias, bias], axis=1)
    pair = lambda refs, w: jnp.concatenate([refs[2 * w][...].astype(BF16), refs[2 * w + 1][...].astype(BF16)],
                                           axis=1)
    order = list(reversed(range(pp // 2)))
    zs = [jnp.dot(qbd, pair(k_refs, w), preferred_element_type=F32) + bias2 for w in order]
    sp_all = jnp.concatenate([_softplus2(z) for z in zs], axis=0).astype(BF16)
    incl_all = jnp.dot(sp_all, negu, preferred_element_type=F32)
    r = r_ref[...]
    acc = acc_ref[...]
    for n, w in enumerate(order):
        incl = incl_all[n * rows:(n + 1) * rows]
        a = jnp.exp2(zs[n] + incl + jnp.concatenate([r, r], axis=1)).astype(BF16)
        acc = acc + _mm_nt(pair(v_refs, w), a)
        r = r + jnp.broadcast_to(incl[:, 0:1], r.shape)
    acc_ref[...] = acc
    r_ref[...] = r

    @pl.when(g == pl.num_programs(1) - 1)
    def _():
        width = acc_ref.shape[0]
        eye = (lax.broadcasted_iota(jnp.int32, (rows, rows), 0)
               == lax.broadcasted_iota(jnp.int32, (rows, rows), 1)).astype(F32).astype(BF16)
        acc_t = sum(_mm_nt(eye, piece) for piece in _split3(acc_ref[...]))
        lane_h = lax.broadcasted_iota(jnp.int32, (tnew, width), 1) // SB_HEAD_DIM
        o = jnp.zeros((tnew, width), F32)
        for h in range(heads):
            o = o + jnp.where(lane_h == h, acc_t[h * tnew:(h + 1) * tnew], 0.0)
        o_ref[...] = (o * g_ref[...].astype(F32)).astype(BF16)


def _sb_paged(page_table, bias, q, k_new, v_new, gate, kpool, vpool, *, pp):
    n, tnew, w = q.shape
    heads = w // SB_HEAD_DIM
    npages = page_table.shape[1]
    page = kpool.shape[2]
    assert npages % pp == 0 and pp % 2 == 0 and tnew <= page and tnew % SUBLANES == 0
    rows = heads * tnew
    head_of_lane = jnp.arange(w) // SB_HEAD_DIM
    sel = (head_of_lane[None, :] == jnp.arange(heads)[:, None])
    qbd = jnp.where(sel[None, :, None, :], q[:, None, :, :], 0).reshape(n, rows, w)
    pad = ((0, 0), (0, page - tnew), (0, 0))
    kn = jnp.pad(k_new, pad)
    vn = jnp.swapaxes(jnp.pad(v_new, pad), 1, 2)
    bias_rep = jnp.broadcast_to(jnp.repeat(bias.astype(F32) * LOG2E, tnew)[:, None], (rows, page))
    negu = _neg_suffix(2 * page)

    def page_map(b, g, pt, *, r):
        return (pt[b, npages - (g + 1) * pp + r], 0, 0)

    per_seq = lambda shape: pl.BlockSpec((None,) + shape, lambda b, g, pt: (b, 0, 0))
    const = lambda a: pl.BlockSpec(a.shape, lambda b, g, pt: (0,) * a.ndim)
    page_specs = [pl.BlockSpec((None, w, page), functools.partial(page_map, r=r)) for r in range(pp)]
    grid_spec = pltpu.PrefetchScalarGridSpec(
        num_scalar_prefetch=1, grid=(n, npages // pp),
        in_specs=[per_seq((rows, w)), per_seq((page, w)), per_seq((w, page)), per_seq((tnew, w)),
                  const(bias_rep), const(negu)] + page_specs + page_specs,
        out_specs=per_seq((tnew, w)),
        scratch_shapes=[pltpu.VMEM((w, rows), F32), pltpu.VMEM((rows, page), F32)],
    )
    return pl.pallas_call(
        functools.partial(_sb_paged_kernel, pp=pp, heads=heads, tnew=tnew),
        grid_spec=grid_spec,
        out_shape=jax.ShapeDtypeStruct((n, tnew, w), BF16),
        compiler_params=_cparams(("parallel", "arbitrary")), name="sb_paged",
    )(page_table, qbd, kn, vn, gate, bias_rep, negu, *([kpool] * pp), *([vpool] * pp))


def _causal_conv_rows(x, prev8, w, width):
    rows = x.shape[0]
    xx = jnp.concatenate([prev8, x], axis=0)
    y = x * w[width - 1:width, :]
    for s in range(1, width):
        y = y + pltpu.roll(xx, s, 0)[SUBLANES:SUBLANES + rows] * w[width - 1 - s:width - s, :]
    return y


def _transpose_exact(x):
    r = lax.broadcasted_iota(jnp.int32, (LANES, LANES), 0)
    c = lax.broadcasted_iota(jnp.int32, (LANES, LANES), 1)
    eye = jnp.where(r == c, 1.0, 0.0).astype(BF16)
    hi, mid, lo = _split3(x)
    return _mm_nt(eye, hi) + (_mm_nt(eye, mid) + _mm_nt(eye, lo))


def _gdn_kernel(x_ref, tail_ref, gate_ref, prev_ref, s0_ref, cw_ref, gp_ref, onw_ref,
                o_ref, sout_ref, s_ref, carry_ref, *, chunk):
    b = pl.program_id(1)
    heads, dk = GDN_HEADS, GDN_DK
    sq, tb = x_ref.shape[0], x_ref.shape[1]
    nc = tb // chunk
    qkw = heads * dk

    @pl.when(b == 0)
    def _():
        s_ref[...] = s0_ref[...]
        carry_ref[...] = prev_ref[...]

    cw = cw_ref[...]
    gp = gp_ref[...]
    onw = onw_ref[...]
    ri = lax.broadcasted_iota(jnp.int32, (tb, tb), 0)
    ci = lax.broadcasted_iota(jnp.int32, (tb, tb), 1)
    same = (ri // chunk) == (ci // chunk)
    incl = same & (ri >= ci)
    strict = same & (ri > ci)
    eye = jnp.where(ri == ci, 1.0, 0.0)
    sum_lhs = jnp.concatenate([jnp.where(incl, 1.0, 0.0), jnp.where(same, 1.0, 0.0)], axis=0)

    beta_all, gc_all, gl_all, gc_t = [], [], [], []
    for i in range(sq):
        tl = tail_ref[i]
        beta_all.append(1.0 / (1.0 + jnp.exp(-tl)))
        xg = tl + gp[1:2, :]
        g_all = -jnp.exp(gp[0:1, :]) * (jnp.maximum(xg, 0.0) + jnp.log(1.0 + jnp.exp(-jnp.abs(xg))))
        sums = _mm_exact_lhs(sum_lhs, g_all)
        gc_all.append(sums[:tb])
        gl_all.append(sums[tb:])
        gc_t.append(_transpose_exact(sums[:tb]))

    chains = [(i, h) for i in range(sq) for h in range(heads)]
    each = lambda f: [f(n) for n in range(len(chains))]

    def conv_cols(i, col0):
        cols = slice(col0, col0 + dk)
        return _silu(_causal_conv_rows(x_ref[i, :, cols], carry_ref[i, :, cols], cw[:, cols], GDN_CONV))

    q = [conv_cols(i, h * dk) for i, h in chains]
    k = [conv_cols(i, qkw + h * dk) for i, h in chains]
    v = [conv_cols(i, 2 * qkw + h * dk) for i, h in chains]
    q = each(lambda n: q[n] * (lax.rsqrt(jnp.sum(q[n] * q[n], -1, keepdims=True) + NORM_EPS) * dk ** -0.5))
    k = each(lambda n: k[n] * lax.rsqrt(jnp.sum(k[n] * k[n], -1, keepdims=True) + NORM_EPS))
    beta = [beta_all[i][:, h:h + 1] for i, h in chains]
    gc = [gc_all[i][:, heads + h:heads + h + 1] for i, h in chains]
    gl = [gl_all[i][:, heads + h:heads + h + 1] for i, h in chains]
    decay = [jnp.exp(jnp.where(incl, gc[n] - jnp.broadcast_to(gc_t[i][heads + h:heads + h + 1, :], (tb, tb)), 0.0))
             for n, (i, h) in enumerate(chains)]
    kb = each(lambda n: k[n] * beta[n])
    kk = each(lambda n: _mm_nt(kb[n], k[n]))
    qk = each(lambda n: _mm_nt(q[n], k[n]))
    m = each(lambda n: jnp.where(strict, kk[n] * decay[n], 0.0))
    qk = each(lambda n: jnp.where(incl, qk[n] * decay[n], 0.0))
    egc = each(lambda n: jnp.exp(gc[n]))
    rhs = each(lambda n: jnp.concatenate([v[n] * beta[n], kb[n] * egc[n]], axis=1))
    inv = each(lambda n: eye - m[n])
    p = m
    for _ in range(max(0, int(math.log2(chunk)) - 1)):
        p = each(lambda n: _mm(p[n], p[n]))
        inv = each(lambda n: inv[n] + _mm(inv[n], p[n]))
    uw = each(lambda n: _mm(inv[n], rhs[n]))
    res = each(lambda n: rhs[n] - uw[n] - _mm3(m[n], uw[n]))
    uw = each(lambda n: uw[n] + _mm(inv[n], res[n]))
    qe = each(lambda n: q[n] * egc[n])
    k_dec = each(lambda n: k[n] * jnp.exp(gl[n] - gc[n]))
    s_dec = each(lambda n: jnp.exp(gl[n]))

    s = [s_ref[i, h] for i, h in chains]
    v_parts = [[] for _ in chains]
    o_parts = [[] for _ in chains]
    for c in range(nc):
        rows = slice(c * chunk, (c + 1) * chunk)
        v_new = each(lambda n: uw[n][rows, :dk] - _mm(uw[n][rows, dk:], s[n]))
        for n in range(len(chains)):
            v_parts[n].append(v_new[n])
        pad = [jnp.zeros(((nc - 1 - c) * chunk, dk), F32)] * (c < nc - 1)
        o_c = each(lambda n: _mm(qe[n][rows], s[n]) + _mm(qk[n][rows], jnp.concatenate(v_parts[n] + pad, axis=0)))
        for n in range(len(chains)):
            o_parts[n].append(o_c[n])
        s = each(lambda n: s[n] * s_dec[n][c * chunk:c * chunk + 1, :] + lax.dot_general(
            k_dec[n][rows].astype(BF16), v_new[n].astype(BF16), (((0,), (0,)), ((), ())),
            preferred_element_type=F32))
    for n, (i, h) in enumerate(chains):
        s_ref[i, h] = s[n]
        o = jnp.concatenate(o_parts[n], axis=0)
        on = o * lax.rsqrt(jnp.mean(o * o, -1, keepdims=True) + NORM_EPS) * onw
        gate = gate_ref[i, :, h * dk:(h + 1) * dk].astype(F32)
        o_ref[i, :, h * dk:(h + 1) * dk] = (on * gate).astype(BF16)
    carry_ref[...] = x_ref[:, tb - SUBLANES:, :]

    @pl.when(b == pl.num_programs(1) - 1)
    def _():
        sout_ref[...] = s_ref[...]


def _gdn(gqkv, tail, gate, prev8, s0, conv_w, gparams, out_norm_w, *, tb, chunk, sq):
    n, t, w = gqkv.shape
    heads, dk = GDN_HEADS, GDN_DK
    assert t % tb == 0 and tb % chunk == 0 and tb % SUBLANES == 0 and n % sq == 0
    tok = lambda width: pl.BlockSpec((sq, tb, width), lambda i, b: (i, b, 0))
    const = lambda a: pl.BlockSpec(a.shape, lambda i, b: (0,) * a.ndim)
    state = pl.BlockSpec((sq, heads, dk, dk), lambda i, b: (i, 0, 0, 0))
    return pl.pallas_call(
        functools.partial(_gdn_kernel, chunk=chunk),
        grid=(n // sq, t // tb),
        in_specs=[tok(w), tok(LANES), tok(heads * dk),
                  pl.BlockSpec((sq, SUBLANES, w), lambda i, b: (i, 0, 0)), state,
                  const(conv_w), const(gparams), const(out_norm_w)],
        out_specs=(tok(heads * dk), state),
        out_shape=(jax.ShapeDtypeStruct((n, t, heads * dk), BF16),
                   jax.ShapeDtypeStruct((n, heads, dk, dk), F32)),
        scratch_shapes=[pltpu.VMEM((sq, heads, dk, dk), F32), pltpu.VMEM((sq, SUBLANES, w), F32)],
        compiler_params=_cparams(("parallel", "arbitrary")), name="gdn",
    )(gqkv, tail, gate, prev8, s0, conv_w, gparams, out_norm_w)


def _tail_kernel(x_ref, ma_ref, mb_ref, prev_ref, wo_ref, nw_ref, win_ref, cw_ref, wout_ref, fnw_ref,
                 y_ref, last_ref, carry_ref, *, tt):
    b = pl.program_id(1)
    rows, d = x_ref.shape
    sa = ma_ref.shape[1]

    @pl.when(b == 0)
    def _():
        carry_ref[...] = prev_ref[...]

    h1 = x_ref[...] + (jnp.dot(ma_ref[...], wo_ref[:sa, :], preferred_element_type=F32)
                       + jnp.dot(mb_ref[...], wo_ref[sa:, :], preferred_element_type=F32))
    xn = _rms(h1, nw_ref[...]).astype(BF16)
    proj = lambda i: jnp.dot(xn, win_ref[:, i * d:(i + 1) * d], preferred_element_type=F32)
    cu = proj(2) * proj(0)
    cw = cw_ref[...]
    if tt == rows:
        yc = _causal_conv_rows(cu, carry_ref[...], cw, SC_CONV)
        carry_ref[...] = cu[rows - SUBLANES:, :]
        last_ref[...] = cu[rows - SUBLANES:, :]
    else:
        assert tt == SUBLANES
        prev = carry_ref[...]
        tok = lax.broadcasted_iota(jnp.int32, (rows, d), 0) % tt
        yc = cu * cw[SC_CONV - 1:SC_CONV, :]
        for s in range(1, SC_CONV):
            shifted = jnp.where(tok >= s, pltpu.roll(cu, s, 0), pltpu.roll(prev, (s - tt) % rows, 0))
            yc = yc + shifted * cw[SC_CONV - 1 - s:SC_CONV - s, :]
        last_ref[...] = cu
    m = (proj(1) * yc * _silu(proj(3))).astype(BF16)
    h2 = h1 + jnp.dot(m, wout_ref[...], preferred_element_type=F32)
    y_ref[...] = _rms(h2, fnw_ref[...])


def _tail(x, mix_a, mix_b, prev, w_out0, norm_w1, sc_w_in, sc_conv_w, sc_w_out, final_norm_w, *, tm, tt):
    g, r, d = x.shape
    assert r % tm == 0 and tm % tt == 0 and (tt == tm or (tt == SUBLANES and r == tm))
    nseq = tm // tt
    tok = lambda width: pl.BlockSpec((None, tm, width), lambda i, b: (i, b, 0))
    const = lambda a: pl.BlockSpec(a.shape, lambda i, b: (0,) * a.ndim, pipeline_mode=pl.Buffered(1))
    small = pl.BlockSpec((None, nseq * SUBLANES, d), lambda i, b: (i, 0, 0))
    return pl.pallas_call(
        functools.partial(_tail_kernel, tt=tt),
        grid=(g, r // tm),
        in_specs=[tok(d), tok(mix_a.shape[2]), tok(mix_b.shape[2]), small,
                  const(w_out0), const(norm_w1), const(sc_w_in), const(sc_conv_w), const(sc_w_out),
                  const(final_norm_w)],
        out_specs=(tok(d), small),
        out_shape=(jax.ShapeDtypeStruct((g, r, d), F32),
                   jax.ShapeDtypeStruct((g, nseq * SUBLANES, d), F32)),
        scratch_shapes=[pltpu.VMEM((nseq * SUBLANES, d), F32)],
        compiler_params=_cparams(("parallel", "arbitrary")), name="tail",
    )(x, mix_a, mix_b, prev, w_out0, norm_w1, sc_w_in, sc_conv_w, sc_w_out, final_norm_w)


def _pad_front_rows(buf, rows):
    return jnp.pad(buf, ((0, 0), (rows - buf.shape[1], 0), (0, 0)))


def _trunk(x, weights, sb_fn, gdn_state, gdn_buf, sc_buf, *, kv_transposed, tm_proj, gdn_tb, gdn_chunk, gdn_sq,
           tail_tm, tail_tt):
    (norm_w, w_main, w_kvt, w_tail, conv_w, gparams, out_norm_w, w_out0, sc_w_in, sc_conv_w, sc_w_out,
     final_norm_w) = weights
    n, t, d = x.shape
    sbw = w_out0.shape[0] // 2
    heads = sbw // SB_HEAD_DIM
    gvw = GDN_HEADS * GDN_DK
    gqkvw = conv_w.shape[1]
    xp = x if kv_transposed else x.reshape(1, n * t, d)
    q, kb, vb, kf, vf, sg, gqkv, gg, tail = _proj0(
        xp, norm_w[0:1], w_main, w_kvt, w_tail, sbw=sbw, gqkvw=gqkvw, gvw=gvw, tm=tm_proj,
        kv_transposed=kv_transposed)
    r3 = lambda a: a.reshape(n, t, a.shape[-1])
    if kv_transposed:
        new_kv = lambda a: jnp.transpose(a.reshape(n, heads, SB_HEAD_DIM, t), (0, 3, 1, 2))[None]
        mix_a = sb_fn(q, kb, vb, sg)
    else:
        new_kv = lambda a: a.reshape(1, n, t, heads, SB_HEAD_DIM)
        mix_a = sb_fn(r3(q), r3(kb), r3(vb), r3(sg))
    new_k, new_v = new_kv(kf), new_kv(vf)
    gqkv3 = r3(gqkv)
    mix_b, new_state = _gdn(gqkv3, r3(tail), r3(gg), _pad_front_rows(gdn_buf, SUBLANES), gdn_state,
                            conv_w, gparams, out_norm_w, tb=gdn_tb, chunk=gdn_chunk, sq=gdn_sq)
    prev_sc = _pad_front_rows(sc_buf, SUBLANES)
    if tail_tt == tail_tm:
        groups = (n, t)
    else:
        groups = (1, n * t)
        prev_sc = prev_sc.reshape(1, n * SUBLANES, d)
    shp = lambda a: a.reshape(groups + (a.shape[-1],))
    y, last = _tail(shp(x), shp(mix_a), shp(mix_b), prev_sc, w_out0, norm_w[1:2], sc_w_in, sc_conv_w,
                    sc_w_out, final_norm_w, tm=tail_tm, tt=tail_tt)
    new_gbuf = gqkv3[:, t - (GDN_CONV - 1):, :][None]
    new_scbuf = last.reshape(n, SUBLANES, d)[:, SUBLANES - (SC_CONV - 1):, :][None]
    return y.reshape(n, t, d), new_k, new_v, new_state[None], new_gbuf, new_scbuf


def kernel(x_prompt, x_sample, cache_sb_k, cache_sb_v, state_gdn, state_gdn_conv, state_sconv, page_table,
           norm_w, ab_w_in, ab_sb_bias, ab_conv_w, ab_a_log, ab_dt_bias, ab_out_norm_w, ab_w_out,
           sc_w_in, sc_conv_w, sc_w_out, final_norm_w):
    assert ab_w_in.shape[0] == 1 and sc_w_in.shape[0] == 1, "one layer of each kind"
    nb, t, d = x_prompt.shape
    ns, ts, _ = x_sample.shape
    heads = GDN_HEADS
    main_cols = ab_w_in.shape[2] - 2 * heads
    sbw = ab_w_out.shape[1] // 2
    w_main = ab_w_in[0, :, :main_cols].astype(BF16)
    w_kvt = ab_w_in[0, :, sbw:3 * sbw].T.astype(BF16)
    w_tail = jnp.pad(ab_w_in[0, :, main_cols:], ((0, 0), (0, LANES - 2 * heads))).astype(BF16)
    gparams = jnp.zeros((SUBLANES, LANES), F32)
    gparams = gparams.at[0, heads:2 * heads].set(ab_a_log[0]).at[1, heads:2 * heads].set(ab_dt_bias[0])
    weights = (norm_w, w_main, w_kvt, w_tail, ab_conv_w[0], gparams, ab_out_norm_w[0:1],
               ab_w_out[0].astype(BF16), sc_w_in[0].astype(BF16), sc_conv_w[0], sc_w_out[0].astype(BF16),
               final_norm_w[None])
    bias = ab_sb_bias[0]

    zeros = lambda *s: jnp.zeros(s, F32)
    tm = min(512, t)
    y_p, k_p, v_p, s_p, gb_p, sc_p = _trunk(
        x_prompt, weights, functools.partial(_sb_prompt, bias, tq=min(256, t), gp=4),
        zeros(nb, heads, GDN_DK, GDN_DK), zeros(nb, GDN_CONV - 1, ab_conv_w.shape[2]),
        zeros(nb, SC_CONV - 1, d),
        kv_transposed=True, tm_proj=tm, gdn_tb=min(256, t), gdn_chunk=min(64, t), gdn_sq=2, tail_tm=tm, tail_tt=tm)

    n_pool, page = cache_sb_k.shape[1], cache_sb_k.shape[2]
    pool_t = lambda c: jnp.transpose(c, (0, 1, 3, 4, 2)).reshape(n_pool, -1, page)
    kpool = pool_t(cache_sb_k)
    vpool = pool_t(cache_sb_v)
    sb_sample = lambda q, kb, vb, sg: _sb_paged(page_table, bias, q, kb, vb, sg, kpool, vpool,
                                                pp=min(16, page_table.shape[1]))
    y_s, k_s, v_s, s_s, gb_s, sc_s = _trunk(
        x_sample, weights, sb_sample, state_gdn[0], state_gdn_conv[0], state_sconv[0],
        kv_transposed=False, tm_proj=ns * ts, gdn_tb=ts, gdn_chunk=ts, gdn_sq=2, tail_tm=ns * ts, tail_tt=ts)
    return (y_p, y_s, k_p, v_p, k_s, v_s, s_p, s_s, gb_p, gb_s, sc_p, sc_s)
```

```python
import functools
import math

import jax
import jax.numpy as jnp
from jax import lax
from jax.experimental import pallas as pl
from jax.experimental.pallas import tpu as pltpu

F32 = jnp.float32
BF16 = jnp.bfloat16
LOG2E = 1.4426950408889634
NORM_EPS = 1e-6
LANES = 128
SUBLANES = 8
VMEM_LIMIT_BYTES = 56 * 1024 * 1024

SB_HEAD_DIM = 64
GDN_HEADS = 4
GDN_DK = 128
GDN_CONV = 4
SC_CONV = 3


def _cparams(sem):
    return pltpu.CompilerParams(dimension_semantics=sem, vmem_limit_bytes=VMEM_LIMIT_BYTES)


def _silu(x):
    return x * (1.0 / (1.0 + jnp.exp(-x)))


def _rms(x, w):
    return x * lax.rsqrt(jnp.mean(x * x, axis=-1, keepdims=True) + NORM_EPS) * w


def _mm(a, b):
    return jnp.dot(a.astype(BF16), b.astype(BF16), preferred_element_type=F32)


def _mm_nt(a, b):
    return lax.dot_general(a.astype(BF16), b.astype(BF16), (((1,), (1,)), ((), ())),
                           preferred_element_type=F32)


def _split3(x):
    hi = x.astype(BF16)
    r = x - hi.astype(F32)
    mid = r.astype(BF16)
    lo = (r - mid.astype(F32)).astype(BF16)
    return hi, mid, lo


def _mm3(a, b):
    ah, am, _ = _split3(a)
    bh, bm, _ = _split3(b)
    d = functools.partial(jnp.dot, preferred_element_type=F32)
    return d(ah, bh) + (d(ah, bm) + d(am, bh))


def _mm_exact_lhs(a01, b):
    a = a01.astype(BF16)
    bh, bm, bl = _split3(b)
    d = functools.partial(jnp.dot, preferred_element_type=F32)
    return d(a, bh) + (d(a, bm) + d(a, bl))


def _softplus2(z):
    neg_abs = pltpu.bitcast(pltpu.bitcast(z, jnp.uint32) | jnp.uint32(0x80000000), F32)
    return jnp.maximum(z, 0.0) + jnp.log2(1.0 + jnp.exp2(neg_abs))


def _proj0_kernel(x_ref, nw_ref, w_ref, wkvt_ref, wt_ref,
                  q_ref, kb_ref, vb_ref, kf_ref, vf_ref, sg_ref, gqkv_ref, gg_ref, tail_ref,
                  *, sbw, gqkvw, gvw, qscale, kv_transposed):
    x = x_ref[...]
    xn = _rms(x, nw_ref[...]).astype(BF16)

    def proj(lo, width):
        return jnp.dot(xn, w_ref[:, lo:lo + width], preferred_element_type=F32)

    q_ref[...] = (proj(0, sbw) * qscale).astype(BF16)
    if kv_transposed:
        nt = lambda wt: lax.dot_general(wt, xn, (((1,), (1,)), ((), ())), preferred_element_type=F32)
        k = nt(wkvt_ref[:sbw, :])
        v = nt(wkvt_ref[sbw:, :])
    else:
        k = proj(sbw, sbw)
        v = proj(2 * sbw, sbw)
    kf_ref[...] = k
    kb_ref[...] = k.astype(BF16)
    vf_ref[...] = v
    vb_ref[...] = v.astype(BF16)
    sg_ref[...] = _silu(proj(3 * sbw, sbw)).astype(BF16)
    gqkv_ref[...] = proj(4 * sbw, gqkvw)
    gg_ref[...] = _silu(proj(4 * sbw + gqkvw, gvw)).astype(BF16)
    tail_ref[...] = jnp.dot(xn, wt_ref[...], preferred_element_type=F32)


def _proj0(x, norm_w, w_main, w_kvt, w_tail, *, sbw, gqkvw, gvw, tm, kv_transposed):
    g, rows, d = x.shape
    assert rows % tm == 0
    qscale = LOG2E * SB_HEAD_DIM ** -0.5
    kern = functools.partial(_proj0_kernel, sbw=sbw, gqkvw=gqkvw, gvw=gvw, qscale=qscale,
                             kv_transposed=kv_transposed)
    row = lambda w: pl.BlockSpec((None, tm, w), lambda b, i: (b, i, 0))
    full = lambda a: pl.BlockSpec(a.shape, lambda b, i: (0,) * a.ndim, pipeline_mode=pl.Buffered(1))
    rs = lambda w, dt: jax.ShapeDtypeStruct((g, rows, w), dt)
    if kv_transposed:
        kv_spec = pl.BlockSpec((None, sbw, tm), lambda b, i: (b, 0, i))
        kv_shape = lambda dt: jax.ShapeDtypeStruct((g, sbw, rows), dt)
    else:
        kv_spec = row(sbw)
        kv_shape = lambda dt: rs(sbw, dt)
    out_shapes = (
        rs(sbw, BF16),
        kv_shape(BF16),
        kv_shape(BF16),
        kv_shape(F32),
        kv_shape(F32),
        rs(sbw, BF16),
        rs(gqkvw, F32),
        rs(gvw, BF16),
        rs(LANES, F32),
    )
    out_specs = (row(sbw), kv_spec, kv_spec, kv_spec, kv_spec, row(sbw), row(gqkvw), row(gvw), row(LANES))
    return pl.pallas_call(
        kern, grid=(g, rows // tm),
        in_specs=[row(d), full(norm_w), full(w_main), full(w_kvt), full(w_tail)],
        out_specs=out_specs, out_shape=out_shapes,
        compiler_params=_cparams(("parallel", "parallel")), name="proj0",
    )(x, norm_w, w_main, w_kvt, w_tail)


def _neg_suffix(tk):
    j = lax.broadcasted_iota(jnp.int32, (tk, tk), 0)
    s = lax.broadcasted_iota(jnp.int32, (tk, tk), 1)
    return jnp.where(j >= s, -1.0, 0.0).astype(BF16)


def _sb_tile(z, r_rep, negu, mask):
    tk = z.shape[1]
    sp = _softplus2(z)
    if mask is not None:
        sp = jnp.where(mask, sp, 0.0)
    incl = jnp.dot(sp.astype(BF16), negu, preferred_element_type=F32)
    r_all = r_rep if tk == LANES else jnp.concatenate([r_rep] * (tk // LANES), axis=1)
    a = jnp.exp2(z + incl + r_all)
    if mask is not None:
        a = jnp.where(mask, a, 0.0)
    return a.astype(BF16), r_rep + jnp.broadcast_to(incl[:, 0:1], r_rep.shape)


BIAS_PIECES = 3


def _sb_prompt_kernel(bias_ref, q_ref, k_ref, v_ref, g_ref, negu_ref, o_ref, kaug_ref, acc_ref, r_ref, z_ref,
                      *, tq, gp):
    pg = pl.program_id(1)
    i = pl.program_id(2)
    hd = SB_HEAD_DIM

    @pl.when(i == 0)
    def _():
        rown = lax.broadcasted_iota(jnp.int32, (LANES, k_ref.shape[1]), 0)
        ones_rows = jnp.where(rown < BIAS_PIECES, 1.0, 0.0).astype(BF16)
        for g in range(gp):
            kaug_ref[g, :LANES, :] = k_ref[g * LANES:(g + 1) * LANES, :]
            kaug_ref[g, LANES:, :] = ones_rows

    lane = lax.broadcasted_iota(jnp.int32, (tq, LANES), 1)
    zero = jnp.zeros((tq, LANES), BF16)

    def bias_lanes(b):
        pieces = _split3(jnp.full((tq, LANES), b * LOG2E, F32))
        out = jnp.zeros((tq, LANES), F32)
        for n, piece in enumerate(pieces):
            out = jnp.where(lane == n, piece.astype(F32), out)
        return out.astype(BF16)

    def stacked_q(g):
        q = q_ref[:, g * LANES:(g + 1) * LANES]
        h0 = 2 * (pg * gp + g)
        return jnp.concatenate(
            [jnp.concatenate([jnp.where(lane < hd, q, zero), bias_lanes(bias_ref[h0])], axis=1),
             jnp.concatenate([jnp.where(lane >= hd, q, zero), bias_lanes(bias_ref[h0 + 1])], axis=1)],
            axis=0)

    q2 = [stacked_q(g) for g in range(gp)]
    negu = negu_ref[...]

    def span(j):
        return pl.ds(pl.multiple_of(j * tq, tq), tq)

    def logits(g, j):
        return jnp.dot(q2[g], kaug_ref[g, :, span(j)], preferred_element_type=F32)

    def attend(g, j, a):
        return _mm_nt(a, v_ref[g * LANES:(g + 1) * LANES, span(j)])

    for g in range(gp):
        z_ref[0, g] = logits(g, jnp.maximum(i - 1, 0))

    row = lax.broadcasted_iota(jnp.int32, (2 * tq, tq), 0)
    col = lax.broadcasted_iota(jnp.int32, (2 * tq, tq), 1)
    diag_mask = col < jnp.where(row >= tq, row - tq, row)
    zs = [logits(g, i) for g in range(gp)]
    for g in range(gp):
        a, r = _sb_tile(zs[g], jnp.zeros((2 * tq, LANES), F32), negu, diag_mask)
        acc_ref[g] = attend(g, i, a)
        r_ref[g] = r

    def step(j, slot, prefetch=True):
        for g in range(gp if prefetch else 0):
            z_ref[1 - slot, g] = logits(g, jnp.maximum(j - 1, 0))
        for g in range(gp):
            a, r = _sb_tile(z_ref[slot, g], r_ref[g], negu, None)
            acc_ref[g] += attend(g, j, a)
            r_ref[g] = r

    def body(jj, carry):
        j = i - 1 - 2 * jj
        step(j, 0)
        step(j - 1, 1)
        return carry

    lax.fori_loop(0, i // 2, body, 0)

    @pl.when(i % 2 == 1)
    def _():
        step(0, 0, prefetch=False)
    lane_o = lax.broadcasted_iota(jnp.int32, (tq, 2 * hd), 1)
    for g in range(gp):
        acc = acc_ref[g]
        o = jnp.where(lane_o < hd, acc[:tq], acc[tq:])
        cols = slice(g * LANES, (g + 1) * LANES)
        o_ref[:, cols] = (o * g_ref[:, cols].astype(F32)).astype(BF16)


def _sb_prompt(bias, q, kt, vt, gate, *, tq, gp):
    n, t, w = q.shape
    gw = gp * LANES
    assert w % gw == 0 and t % tq == 0 and tq % LANES == 0
    negu = _neg_suffix(tq)
    blk = pl.BlockSpec((None, tq, gw), lambda b, p, i: (b, i, p))
    seq = pl.BlockSpec((None, gw, t), lambda b, p, i: (b, p, 0))
    return pl.pallas_call(
        functools.partial(_sb_prompt_kernel, tq=tq, gp=gp),
        grid=(n, w // gw, t // tq),
        in_specs=[pl.BlockSpec(memory_space=pltpu.SMEM), blk, seq, seq, blk,
                  pl.BlockSpec(negu.shape, lambda b, p, i: (0, 0))],
        out_specs=blk,
        out_shape=jax.ShapeDtypeStruct((n, t, w), BF16),
        scratch_shapes=[pltpu.VMEM((gp, 2 * LANES, t), BF16), pltpu.VMEM((gp, 2 * tq, LANES), F32),
                        pltpu.VMEM((gp, 2 * tq, LANES), F32), pltpu.VMEM((2, gp, 2 * tq, tq), F32)],
        compiler_params=_cparams(("parallel", "parallel", "arbitrary")), name="sb_prompt",
    )(bias, q, kt, vt, gate, negu)


def _sb_paged_kernel(pt_ref, qbd_ref, kn_ref, vn_ref, g_ref, bias_ref, negu_ref, *rest, pp, heads, tnew):
    k_refs = rest[:pp]
    v_refs = rest[pp:2 * pp]
    o_ref, acc_ref, r_ref = rest[2 * pp:]
    g = pl.program_id(1)
    qbd = qbd_ref[...]
    bias = bias_ref[...]
    negu = negu_ref[...]
    rows, page = bias.shape

    @pl.when(g == 0)
    def _():
        z = _mm_nt(qbd, kn_ref[...]) + bias
        row = lax.broadcasted_iota(jnp.int32, (rows, page), 0)
        col = lax.broadcasted_iota(jnp.int32, (rows, page), 1)
        mask = col < (row % tnew)
        a, r = _sb_tile(z, jnp.zeros((rows, page), F32), negu[:page, :page], mask)
        acc_ref[...] = _mm_nt(vn_ref[...], a)
        r_ref[...] = r

    bias2 = jnp.concatenate([bias, bias], axis=1)
    pair = lambda refs, w: jnp.concatenate([refs[2 * w][...].astype(BF16), refs[2 * w + 1][...].astype(BF16)],
                                           axis=1)
    order = list(reversed(range(pp // 2)))
    zs = [jnp.dot(qbd, pair(k_refs, w), preferred_element_type=F32) + bias2 for w in order]
    sp_all = jnp.concatenate([_softplus2(z) for z in zs], axis=0).astype(BF16)
    incl_all = jnp.dot(sp_all, negu, preferred_element_type=F32)
    r = r_ref[...]
    acc = acc_ref[...]
    for n, w in enumerate(order):
        incl = incl_all[n * rows:(n + 1) * rows]
        a = jnp.exp2(zs[n] + incl + jnp.concatenate([r, r], axis=1)).astype(BF16)
        acc = acc + _mm_nt(pair(v_refs, w), a)
        r = r + jnp.broadcast_to(incl[:, 0:1], r.shape)
    acc_ref[...] = acc
    r_ref[...] = r

    @pl.when(g == pl.num_programs(1) - 1)
    def _():
        width = acc_ref.shape[0]
        eye = (lax.broadcasted_iota(jnp.int32, (rows, rows), 0)
               == lax.broadcasted_iota(jnp.int32, (rows, rows), 1)).astype(F32).astype(BF16)
        acc_t = sum(_mm_nt(eye, piece) for piece in _split3(acc_ref[...]))
        lane_h = lax.broadcasted_iota(jnp.int32, (tnew, width), 1) // SB_HEAD_DIM
        o = jnp.zeros((tnew, width), F32)
        for h in range(heads):
            o = o + jnp.where(lane_h == h, acc_t[h * tnew:(h + 1) * tnew], 0.0)
        o_ref[...] = (o * g_ref[...].astype(F32)).astype(BF16)


def _sb_paged(page_table, bias, q, k_new, v_new, gate, kpool, vpool, *, pp):
    n, tnew, w = q.shape
    heads = w // SB_HEAD_DIM
    npages = page_table.shape[1]
    page = kpool.shape[2]
    assert npages % pp == 0 and pp % 2 == 0 and tnew <= page and tnew % SUBLANES == 0
    rows = heads * tnew
    head_of_lane = jnp.arange(w) // SB_HEAD_DIM
    sel = (head_of_lane[None, :] == jnp.arange(heads)[:, None])
    qbd = jnp.where(sel[None, :, None, :], q[:, None, :, :], 0).reshape(n, rows, w)
    pad = ((0, 0), (0, page - tnew), (0, 0))
    kn = jnp.pad(k_new, pad)
    vn = jnp.swapaxes(jnp.pad(v_new, pad), 1, 2)
    bias_rep = jnp.broadcast_to(jnp.repeat(bias.astype(F32) * LOG2E, tnew)[:, None], (rows, page))
    negu = _neg_suffix(2 * page)

    def page_map(b, g, pt, *, r):
        return (pt[b, npages - (g + 1) * pp + r], 0, 0)

    per_seq = lambda shape: pl.BlockSpec((None,) + shape, lambda b, g, pt: (b, 0, 0))
    const = lambda a: pl.BlockSpec(a.shape, lambda b, g, pt: (0,) * a.ndim)
    page_specs = [pl.BlockSpec((None, w, page), functools.partial(page_map, r=r)) for r in range(pp)]
    grid_spec = pltpu.PrefetchScalarGridSpec(
        num_scalar_prefetch=1, grid=(n, npages // pp),
        in_specs=[per_seq((rows, w)), per_seq((page, w)), per_seq((w, page)), per_seq((tnew, w)),
                  const(bias_rep), const(negu)] + page_specs + page_specs,
        out_specs=per_seq((tnew, w)),
        scratch_shapes=[pltpu.VMEM((w, rows), F32), pltpu.VMEM((rows, page), F32)],
    )
    return pl.pallas_call(
        functools.partial(_sb_paged_kernel, pp=pp, heads=heads, tnew=tnew),
        grid_spec=grid_spec,
        out_shape=jax.ShapeDtypeStruct((n, tnew, w), BF16),
        compiler_params=_cparams(("parallel", "arbitrary")), name="sb_paged",
    )(page_table, qbd, kn, vn, gate, bias_rep, negu, *([kpool] * pp), *([vpool] * pp))


def _causal_conv_rows(x, prev8, w, width):
    rows = x.shape[0]
    xx = jnp.concatenate([prev8, x], axis=0)
    y = x * w[width - 1:width, :]
    for s in range(1, width):
        y = y + pltpu.roll(xx, s, 0)[SUBLANES:SUBLANES + rows] * w[width - 1 - s:width - s, :]
    return y


def _transpose_exact(x):
    r = lax.broadcasted_iota(jnp.int32, (LANES, LANES), 0)
    c = lax.broadcasted_iota(jnp.int32, (LANES, LANES), 1)
    eye = jnp.where(r == c, 1.0, 0.0).astype(BF16)
    hi, mid, lo = _split3(x)
    return _mm_nt(eye, hi) + (_mm_nt(eye, mid) + _mm_nt(eye, lo))


def _gdn_kernel(x_ref, tail_ref, gate_ref, prev_ref, s0_ref, cw_ref, gp_ref, onw_ref,
                o_ref, sout_ref, s_ref, carry_ref, *, chunk):
    b = pl.program_id(1)
    heads, dk = GDN_HEADS, GDN_DK
    sq, tb = x_ref.shape[0], x_ref.shape[1]
    nc = tb // chunk
    qkw = heads * dk

    @pl.when(b == 0)
    def _():
        s_ref[...] = s0_ref[...]
        carry_ref[...] = prev_ref[...]

    cw = cw_ref[...]
    gp = gp_ref[...]
    onw = onw_ref[...]
    ri = lax.broadcasted_iota(jnp.int32, (tb, tb), 0)
    ci = lax.broadcasted_iota(jnp.int32, (tb, tb), 1)
    same = (ri // chunk) == (ci // chunk)
    incl = same & (ri >= ci)
    strict = same & (ri > ci)
    eye = jnp.where(ri == ci, 1.0, 0.0)
    sum_lhs = jnp.concatenate([jnp.where(incl, 1.0, 0.0), jnp.where(same, 1.0, 0.0)], axis=0)

    beta_all, gc_all, gl_all, gc_t = [], [], [], []
    for i in range(sq):
        tl = tail_ref[i]
        beta_all.append(1.0 / (1.0 + jnp.exp(-tl)))
        xg = tl + gp[1:2, :]
        g_all = -jnp.exp(gp[0:1, :]) * (jnp.maximum(xg, 0.0) + jnp.log(1.0 + jnp.exp(-jnp.abs(xg))))
        sums = _mm_exact_lhs(sum_lhs, g_all)
        gc_all.append(sums[:tb])
        gl_all.append(sums[tb:])
        gc_t.append(_transpose_exact(sums[:tb]))

    chains = [(i, h) for i in range(sq) for h in range(heads)]
    each = lambda f: [f(n) for n in range(len(chains))]

    def conv_cols(i, col0):
        cols = slice(col0, col0 + dk)
        return _silu(_causal_conv_rows(x_ref[i, :, cols], carry_ref[i, :, cols], cw[:, cols], GDN_CONV))

    q = [conv_cols(i, h * dk) for i, h in chains]
    k = [conv_cols(i, qkw + h * dk) for i, h in chains]
    v = [conv_cols(i, 2 * qkw + h * dk) for i, h in chains]
    q = each(lambda n: q[n] * (lax.rsqrt(jnp.sum(q[n] * q[n], -1, keepdims=True) + NORM_EPS) * dk ** -0.5))
    k = each(lambda n: k[n] * lax.rsqrt(jnp.sum(k[n] * k[n], -1, keepdims=True) + NORM_EPS))
    beta = [beta_all[i][:, h:h + 1] for i, h in chains]
    gc = [gc_all[i][:, heads + h:heads + h + 1] for i, h in chains]
    gl = [gl_all[i][:, heads + h:heads + h + 1] for i, h in chains]
    decay = [jnp.exp(jnp.where(incl, gc[n] - jnp.broadcast_to(gc_t[i][heads + h:heads + h + 1, :], (tb, tb)), 0.0))
             for n, (i, h) in enumerate(chains)]
    kb = each(lambda n: k[n] * beta[n])
    kk = each(lambda n: _mm_nt(kb[n], k[n]))
    qk = each(lambda n: _mm_nt(q[n], k[n]))
    m = each(lambda n: jnp.where(strict, kk[n] * decay[n], 0.0))
    qk = each(lambda n: jnp.where(incl, qk[n] * decay[n], 0.0))
    egc = each(lambda n: jnp.exp(gc[n]))
    rhs = each(lambda n: jnp.concatenate([v[n] * beta[n], kb[n] * egc[n]], axis=1))
    inv = each(lambda n: eye - m[n])
    p = m
    for _ in range(max(0, int(math.log2(chunk)) - 1)):
        p = each(lambda n: _mm(p[n], p[n]))
        inv = each(lambda n: inv[n] + _mm(inv[n], p[n]))
    uw = each(lambda n: _mm(inv[n], rhs[n]))
    res = each(lambda n: rhs[n] - uw[n] - _mm3(m[n], uw[n]))
    uw = each(lambda n: uw[n] + _mm(inv[n], res[n]))
    qe = each(lambda n: q[n] * egc[n])
    k_dec = each(lambda n: k[n] * jnp.exp(gl[n] - gc[n]))
    s_dec = each(lambda n: jnp.exp(gl[n]))

    s = [s_ref[i, h] for i, h in chains]
    v_parts = [[] for _ in chains]
    o_parts = [[] for _ in chains]
    for c in range(nc):
        rows = slice(c * chunk, (c + 1) * chunk)
        v_new = each(lambda n: uw[n][rows, :dk] - _mm(uw[n][rows, dk:], s[n]))
        for n in range(len(chains)):
            v_parts[n].append(v_new[n])
        pad = [jnp.zeros(((nc - 1 - c) * chunk, dk), F32)] * (c < nc - 1)
        o_c = each(lambda n: _mm(qe[n][rows], s[n]) + _mm(qk[n][rows], jnp.concatenate(v_parts[n] + pad, axis=0)))
        for n in range(len(chains)):
            o_parts[n].append(o_c[n])
        s = each(lambda n: s[n] * s_dec[n][c * chunk:c * chunk + 1, :] + lax.dot_general(
            k_dec[n][rows].astype(BF16), v_new[n].astype(BF16), (((0,), (0,)), ((), ())),
            preferred_element_type=F32))
    for n, (i, h) in enumerate(chains):
        s_ref[i, h] = s[n]
        o = jnp.concatenate(o_parts[n], axis=0)
        on = o * lax.rsqrt(jnp.mean(o * o, -1, keepdims=True) + NORM_EPS) * onw
        gate = gate_ref[i, :, h * dk:(h + 1) * dk].astype(F32)
        o_ref[i, :, h * dk:(h + 1) * dk] = (on * gate).astype(BF16)
    carry_ref[...] = x_ref[:, tb - SUBLANES:, :]

    @pl.when(b == pl.num_programs(1) - 1)
    def _():
        sout_ref[...] = s_ref[...]


def _gdn(gqkv, tail, gate, prev8, s0, conv_w, gparams, out_norm_w, *, tb, chunk, sq):
    n, t, w = gqkv.shape
    heads, dk = GDN_HEADS, GDN_DK
    assert t % tb == 0 and tb % chunk == 0 and tb % SUBLANES == 0 and n % sq == 0
    tok = lambda width: pl.BlockSpec((sq, tb, width), lambda i, b: (i, b, 0))
    const = lambda a: pl.BlockSpec(a.shape, lambda i, b: (0,) * a.ndim)
    state = pl.BlockSpec((sq, heads, dk, dk), lambda i, b: (i, 0, 0, 0))
    return pl.pallas_call(
        functools.partial(_gdn_kernel, chunk=chunk),
        grid=(n // sq, t // tb),
        in_specs=[tok(w), tok(LANES), tok(heads * dk),
                  pl.BlockSpec((sq, SUBLANES, w), lambda i, b: (i, 0, 0)), state,
                  const(conv_w), const(gparams), const(out_norm_w)],
        out_specs=(tok(heads * dk), state),
        out_shape=(jax.ShapeDtypeStruct((n, t, heads * dk), BF16),
                   jax.ShapeDtypeStruct((n, heads, dk, dk), F32)),
        scratch_shapes=[pltpu.VMEM((sq, heads, dk, dk), F32), pltpu.VMEM((sq, SUBLANES, w), F32)],
        compiler_params=_cparams(("parallel", "arbitrary")), name="gdn",
    )(gqkv, tail, gate, prev8, s0, conv_w, gparams, out_norm_w)


def _tail_kernel(x_ref, ma_ref, mb_ref, prev_ref, wo_ref, nw_ref, win_ref, cw_ref, wout_ref, fnw_ref,
                 y_ref, last_ref, carry_ref, *, tt):
    b = pl.program_id(1)
    rows, d = x_ref.shape
    sa = ma_ref.shape[1]

    @pl.when(b == 0)
    def _():
        carry_ref[...] = prev_ref[...]

    h1 = x_ref[...] + (jnp.dot(ma_ref[...], wo_ref[:sa, :], preferred_element_type=F32)
                       + jnp.dot(mb_ref[...], wo_ref[sa:, :], preferred_element_type=F32))
    xn = _rms(h1, nw_ref[...]).astype(BF16)
    proj = lambda i: jnp.dot(xn, win_ref[:, i * d:(i + 1) * d], preferred_element_type=F32)
    cu = proj(2) * proj(0)
    cw = cw_ref[...]
    if tt == rows:
        yc = _causal_conv_rows(cu, carry_ref[...], cw, SC_CONV)
        carry_ref[...] = cu[rows - SUBLANES:, :]
        last_ref[...] = cu[rows - SUBLANES:, :]
    else:
        assert tt == SUBLANES
        prev = carry_ref[...]
        tok = lax.broadcasted_iota(jnp.int32, (rows, d), 0) % tt
        yc = cu * cw[SC_CONV - 1:SC_CONV, :]
        for s in range(1, SC_CONV):
            shifted = jnp.where(tok >= s, pltpu.roll(cu, s, 0), pltpu.roll(prev, (s - tt) % rows, 0))
            yc = yc + shifted * cw[SC_CONV - 1 - s:SC_CONV - s, :]
        last_ref[...] = cu
    m = (proj(1) * yc * _silu(proj(3))).astype(BF16)
    h2 = h1 + jnp.dot(m, wout_ref[...], preferred_element_type=F32)
    y_ref[...] = _rms(h2, fnw_ref[...])


def _tail(x, mix_a, mix_b, prev, w_out0, norm_w1, sc_w_in, sc_conv_w, sc_w_out, final_norm_w, *, tm, tt):
    g, r, d = x.shape
    assert r % tm == 0 and tm % tt == 0 and (tt == tm or (tt == SUBLANES and r == tm))
    nseq = tm // tt
    tok = lambda width: pl.BlockSpec((None, tm, width), lambda i, b: (i, b, 0))
    const = lambda a: pl.BlockSpec(a.shape, lambda i, b: (0,) * a.ndim, pipeline_mode=pl.Buffered(1))
    small = pl.BlockSpec((None, nseq * SUBLANES, d), lambda i, b: (i, 0, 0))
    return pl.pallas_call(
        functools.partial(_tail_kernel, tt=tt),
        grid=(g, r // tm),
        in_specs=[tok(d), tok(mix_a.shape[2]), tok(mix_b.shape[2]), small,
                  const(w_out0), const(norm_w1), const(sc_w_in), const(sc_conv_w), const(sc_w_out),
                  const(final_norm_w)],
        out_specs=(tok(d), small),
        out_shape=(jax.ShapeDtypeStruct((g, r, d), F32),
                   jax.ShapeDtypeStruct((g, nseq * SUBLANES, d), F32)),
        scratch_shapes=[pltpu.VMEM((nseq * SUBLANES, d), F32)],
        compiler_params=_cparams(("parallel", "arbitrary")), name="tail",
    )(x, mix_a, mix_b, prev, w_out0, norm_w1, sc_w_in, sc_conv_w, sc_w_out, final_norm_w)


def _pad_front_rows(buf, rows):
    return jnp.pad(buf, ((0, 0), (rows - buf.shape[1], 0), (0, 0)))


def _trunk(x, weights, sb_fn, gdn_state, gdn_buf, sc_buf, *, kv_transposed, tm_proj, gdn_tb, gdn_chunk, gdn_sq,
           tail_tm, tail_tt):
    (norm_w, w_main, w_kvt, w_tail, conv_w, gparams, out_norm_w, w_out0, sc_w_in, sc_conv_w, sc_w_out,
     final_norm_w) = weights
    n, t, d = x.shape
    sbw = w_out0.shape[0] // 2
    heads = sbw // SB_HEAD_DIM
    gvw = GDN_HEADS * GDN_DK
    gqkvw = conv_w.shape[1]
    xp = x if kv_transposed else x.reshape(1, n * t, d)
    q, kb, vb, kf, vf, sg, gqkv, gg, tail = _proj0(
        xp, norm_w[0:1], w_main, w_kvt, w_tail, sbw=sbw, gqkvw=gqkvw, gvw=gvw, tm=tm_proj,
        kv_transposed=kv_transposed)
    r3 = lambda a: a.reshape(n, t, a.shape[-1])
    if kv_transposed:
        new_kv = lambda a: jnp.transpose(a.reshape(n, heads, SB_HEAD_DIM, t), (0, 3, 1, 2))[None]
        mix_a = sb_fn(q, kb, vb, sg)
    else:
        new_kv = lambda a: a.reshape(1, n, t, heads, SB_HEAD_DIM)
        mix_a = sb_fn(r3(q), r3(kb), r3(vb), r3(sg))
    new_k, new_v = new_kv(kf), new_kv(vf)
    gqkv3 = r3(gqkv)
    mix_b, new_state = _gdn(gqkv3, r3(tail), r3(gg), _pad_front_rows(gdn_buf, SUBLANES), gdn_state,
                            conv_w, gparams, out_norm_w, tb=gdn_tb, chunk=gdn_chunk, sq=gdn_sq)
    prev_sc = _pad_front_rows(sc_buf, SUBLANES)
    if tail_tt == tail_tm:
        groups = (n, t)
    else:
        groups = (1, n * t)
        prev_sc = prev_sc.reshape(1, n * SUBLANES, d)
    shp = lambda a: a.reshape(groups + (a.shape[-1],))
    y, last = _tail(shp(x), shp(mix_a), shp(mix_b), prev_sc, w_out0, norm_w[1:2], sc_w_in, sc_conv_w,
                    sc_w_out, final_norm_w, tm=tail_tm, tt=tail_tt)
    new_gbuf = gqkv3[:, t - (GDN_CONV - 1):, :][None]
    new_scbuf = last.reshape(n, SUBLANES, d)[:, SUBLANES - (SC_CONV - 1):, :][None]
    return y.reshape(n, t, d), new_k, new_v, new_state[None], new_gbuf, new_scbuf


def kernel(x_prompt, x_sample, cache_sb_k, cache_sb_v, state_gdn, state_gdn_conv, state_sconv, page_table,
           norm_w, ab_w_in, ab_sb_bias, ab_conv_w, ab_a_log, ab_dt_bias, ab_out_norm_w, ab_w_out,
           sc_w_in, sc_conv_w, sc_w_out, final_norm_w):
    assert ab_w_in.shape[0] == 1 and sc_w_in.shape[0] == 1, "one layer of each kind"
    nb, t, d = x_prompt.shape
    ns, ts, _ = x_sample.shape
    heads = GDN_HEADS
    main_cols = ab_w_in.shape[2] - 2 * heads
    sbw = ab_w_out.shape[1] // 2
    w_main = ab_w_in[0, :, :main_cols].astype(BF16)
    w_kvt = ab_w_in[0, :, sbw:3 * sbw].T.astype(BF16)
    w_tail = jnp.pad(ab_w_in[0, :, main_cols:], ((0, 0), (0, LANES - 2 * heads))).astype(BF16)
    gparams = jnp.zeros((SUBLANES, LANES), F32)
    gparams = gparams.at[0, heads:2 * heads].set(ab_a_log[0]).at[1, heads:2 * heads].set(ab_dt_bias[0])
    weights = (norm_w, w_main, w_kvt, w_tail, ab_conv_w[0], gparams, ab_out_norm_w[0:1],
               ab_w_out[0].astype(BF16), sc_w_in[0].astype(BF16), sc_conv_w[0], sc_w_out[0].astype(BF16),
               final_norm_w[None])
    bias = ab_sb_bias[0]

    zeros = lambda *s: jnp.zeros(s, F32)
    tm = min(512, t)
    y_p, k_p, v_p, s_p, gb_p, sc_p = _trunk(
        x_prompt, weights, functools.partial(_sb_prompt, bias, tq=min(256, t), gp=4),
        zeros(nb, heads, GDN_DK, GDN_DK), zeros(nb, GDN_CONV - 1, ab_conv_w.shape[2]),
        zeros(nb, SC_CONV - 1, d),
        kv_transposed=True, tm_proj=tm, gdn_tb=min(256, t), gdn_chunk=min(64, t), gdn_sq=4, tail_tm=tm, tail_tt=tm)

    n_pool, page = cache_sb_k.shape[1], cache_sb_k.shape[2]
    pool_t = lambda c: jnp.transpose(c, (0, 1, 3, 4, 2)).reshape(n_pool, -1, page)
    kpool = pool_t(cache_sb_k)
    vpool = pool_t(cache_sb_v)
    sb_sample = lambda q, kb, vb, sg: _sb_paged(page_table, bias, q, kb, vb, sg, kpool, vpool,
                                                pp=min(16, page_table.shape[1]))
    y_s, k_s, v_s, s_s, gb_s, sc_s = _trunk(
        x_sample, weights, sb_sample, state_gdn[0], state_gdn_conv[0], state_sconv[0],
        kv_transposed=False, tm_proj=ns * ts, gdn_tb=ts, gdn_chunk=ts, gdn_sq=2, tail_tm=ns * ts, tail_tt=ts)
    return (y_p, y_s, k_p, v_p, k_s, v_s, s_p, s_s, gb_p, gb_s, sc_p, sc_s)
```

```python
import functools
import math

import jax
import jax.numpy as jnp
from jax import lax
from jax.experimental import pallas as pl
from jax.experimental.pallas import tpu as pltpu

F32 = jnp.float32
BF16 = jnp.bfloat16
LOG2E = 1.4426950408889634
NORM_EPS = 1e-6
LANES = 128
SUBLANES = 8
VMEM_LIMIT_BYTES = 56 * 1024 * 1024

SB_HEAD_DIM = 64
GDN_HEADS = 4
GDN_DK = 128
GDN_CONV = 4
SC_CONV = 3


def _cparams(sem):
    return pltpu.CompilerParams(dimension_semantics=sem, vmem_limit_bytes=VMEM_LIMIT_BYTES)


def _silu(x):
    return x * (1.0 / (1.0 + jnp.exp(-x)))


def _rms(x, w):
    return x * lax.rsqrt(jnp.mean(x * x, axis=-1, keepdims=True) + NORM_EPS) * w


def _mm(a, b):
    return jnp.dot(a.astype(BF16), b.astype(BF16), preferred_element_type=F32)


def _mm_nt(a, b):
    return lax.dot_general(a.astype(BF16), b.astype(BF16), (((1,), (1,)), ((), ())),
                           preferred_element_type=F32)


def _split3(x):
    hi = x.astype(BF16)
    r = x - hi.astype(F32)
    mid = r.astype(BF16)
    lo = (r - mid.astype(F32)).astype(BF16)
    return hi, mid, lo


def _mm3(a, b):
    ah, am, _ = _split3(a)
    bh, bm, _ = _split3(b)
    d = functools.partial(jnp.dot, preferred_element_type=F32)
    return d(ah, bh) + (d(ah, bm) + d(am, bh))


def _mm_exact_lhs(a01, b):
    a = a01.astype(BF16)
    bh, bm, bl = _split3(b)
    d = functools.partial(jnp.dot, preferred_element_type=F32)
    return d(a, bh) + (d(a, bm) + d(a, bl))


def _softplus2(z):
    neg_abs = pltpu.bitcast(pltpu.bitcast(z, jnp.uint32) | jnp.uint32(0x80000000), F32)
    return jnp.maximum(z, 0.0) + jnp.log2(1.0 + jnp.exp2(neg_abs))


def _proj0_kernel(x_ref, nw_ref, w_ref, wkvt_ref, wt_ref,
                  q_ref, kb_ref, vb_ref, kf_ref, vf_ref, sg_ref, gqkv_ref, gg_ref, tail_ref,
                  *, sbw, gqkvw, gvw, qscale, kv_transposed):
    x = x_ref[...]
    xn = _rms(x, nw_ref[...]).astype(BF16)

    def proj(lo, width):
        return jnp.dot(xn, w_ref[:, lo:lo + width], preferred_element_type=F32)

    q_ref[...] = (proj(0, sbw) * qscale).astype(BF16)
    if kv_transposed:
        nt = lambda wt: lax.dot_general(wt, xn, (((1,), (1,)), ((), ())), preferred_element_type=F32)
        k = nt(wkvt_ref[:sbw, :])
        v = nt(wkvt_ref[sbw:, :])
    else:
        k = proj(sbw, sbw)
        v = proj(2 * sbw, sbw)
    kf_ref[...] = k
    kb_ref[...] = k.astype(BF16)
    vf_ref[...] = v
    vb_ref[...] = v.astype(BF16)
    sg_ref[...] = _silu(proj(3 * sbw, sbw)).astype(BF16)
    gqkv_ref[...] = proj(4 * sbw, gqkvw)
    gg_ref[...] = _silu(proj(4 * sbw + gqkvw, gvw)).astype(BF16)
    tail_ref[...] = jnp.dot(xn, wt_ref[...], preferred_element_type=F32)


def _proj0(x, norm_w, w_main, w_kvt, w_tail, *, sbw, gqkvw, gvw, tm, kv_transposed):
    g, rows, d = x.shape
    assert rows % tm == 0
    qscale = LOG2E * SB_HEAD_DIM ** -0.5
    kern = functools.partial(_proj0_kernel, sbw=sbw, gqkvw=gqkvw, gvw=gvw, qscale=qscale,
                             kv_transposed=kv_transposed)
    row = lambda w: pl.BlockSpec((None, tm, w), lambda b, i: (b, i, 0))
    full = lambda a: pl.BlockSpec(a.shape, lambda b, i: (0,) * a.ndim, pipeline_mode=pl.Buffered(1))
    rs = lambda w, dt: jax.ShapeDtypeStruct((g, rows, w), dt)
    if kv_transposed:
        kv_spec = pl.BlockSpec((None, sbw, tm), lambda b, i: (b, 0, i))
        kv_shape = lambda dt: jax.ShapeDtypeStruct((g, sbw, rows), dt)
    else:
        kv_spec = row(sbw)
        kv_shape = lambda dt: rs(sbw, dt)
    out_shapes = (
        rs(sbw, BF16),
        kv_shape(BF16),
        kv_shape(BF16),
        kv_shape(F32),
        kv_shape(F32),
        rs(sbw, BF16),
        rs(gqkvw, F32),
        rs(gvw, BF16),
        rs(LANES, F32),
    )
    out_specs = (row(sbw), kv_spec, kv_spec, kv_spec, kv_spec, row(sbw), row(gqkvw), row(gvw), row(LANES))
    return pl.pallas_call(
        kern, grid=(g, rows // tm),
        in_specs=[row(d), full(norm_w), full(w_main), full(w_kvt), full(w_tail)],
        out_specs=out_specs, out_shape=out_shapes,
        compiler_params=_cparams(("parallel", "parallel")), name="proj0",
    )(x, norm_w, w_main, w_kvt, w_tail)


def _neg_suffix(tk):
    j = lax.broadcasted_iota(jnp.int32, (tk, tk), 0)
    s = lax.broadcasted_iota(jnp.int32, (tk, tk), 1)
    return jnp.where(j >= s, -1.0, 0.0).astype(BF16)


def _sb_tile(z, r_rep, negu, mask):
    tk = z.shape[1]
    sp = _softplus2(z)
    if mask is not None:
        sp = jnp.where(mask, sp, 0.0)
    incl = jnp.dot(sp.astype(BF16), negu, preferred_element_type=F32)
    r_all = r_rep if tk == LANES else jnp.concatenate([r_rep] * (tk // LANES), axis=1)
    a = jnp.exp2(z + incl + r_all)
    if mask is not None:
        a = jnp.where(mask, a, 0.0)
    return a.astype(BF16), r_rep + jnp.broadcast_to(incl[:, 0:1], r_rep.shape)


BIAS_PIECES = 3


def _sb_prompt_kernel(bias_ref, q_ref, k_ref, v_ref, g_ref, negu_ref, o_ref, kaug_ref, acc_ref, r_ref, z_ref,
                      *, tq, gp):
    pg = pl.program_id(1)
    i = pl.program_id(2)
    hd = SB_HEAD_DIM

    @pl.when(i == 0)
    def _():
        rown = lax.broadcasted_iota(jnp.int32, (LANES, k_ref.shape[1]), 0)
        ones_rows = jnp.where(rown < BIAS_PIECES, 1.0, 0.0).astype(BF16)
        for g in range(gp):
            kaug_ref[g, :LANES, :] = k_ref[g * LANES:(g + 1) * LANES, :]
            kaug_ref[g, LANES:, :] = ones_rows

    lane = lax.broadcasted_iota(jnp.int32, (tq, LANES), 1)
    zero = jnp.zeros((tq, LANES), BF16)

    def bias_lanes(b):
        pieces = _split3(jnp.full((tq, LANES), b * LOG2E, F32))
        out = jnp.zeros((tq, LANES), F32)
        for n, piece in enumerate(pieces):
            out = jnp.where(lane == n, piece.astype(F32), out)
        return out.astype(BF16)

    def stacked_q(g):
        q = q_ref[:, g * LANES:(g + 1) * LANES]
        h0 = 2 * (pg * gp + g)
        return jnp.concatenate(
            [jnp.concatenate([jnp.where(lane < hd, q, zero), bias_lanes(bias_ref[h0])], axis=1),
             jnp.concatenate([jnp.where(lane >= hd, q, zero), bias_lanes(bias_ref[h0 + 1])], axis=1)],
            axis=0)

    q2 = [stacked_q(g) for g in range(gp)]
    negu = negu_ref[...]

    def span(j):
        return pl.ds(pl.multiple_of(j * tq, tq), tq)

    def logits(g, j):
        return jnp.dot(q2[g], kaug_ref[g, :, span(j)], preferred_element_type=F32)

    def attend(g, j, a):
        return _mm_nt(a, v_ref[g * LANES:(g + 1) * LANES, span(j)])

    for g in range(gp):
        z_ref[0, g] = logits(g, jnp.maximum(i - 1, 0))

    row = lax.broadcasted_iota(jnp.int32, (2 * tq, tq), 0)
    col = lax.broadcasted_iota(jnp.int32, (2 * tq, tq), 1)
    diag_mask = col < jnp.where(row >= tq, row - tq, row)
    zs = [logits(g, i) for g in range(gp)]
    for g in range(gp):
        a, r = _sb_tile(zs[g], jnp.zeros((2 * tq, LANES), F32), negu, diag_mask)
        acc_ref[g] = attend(g, i, a)
        r_ref[g] = r

    def step(j, slot, prefetch=True):
        for g in range(gp):
            if prefetch:
                z_ref[1 - slot, g] = logits(g, jnp.maximum(j - 1, 0))
            a, r = _sb_tile(z_ref[slot, g], r_ref[g], negu, None)
            acc_ref[g] += attend(g, j, a)
            r_ref[g] = r

    def body(jj, carry):
        j = i - 1 - 2 * jj
        step(j, 0)
        step(j - 1, 1)
        return carry

    lax.fori_loop(0, i // 2, body, 0)

    @pl.when(i % 2 == 1)
    def _():
        step(0, 0, prefetch=False)
    lane_o = lax.broadcasted_iota(jnp.int32, (tq, 2 * hd), 1)
    for g in range(gp):
        acc = acc_ref[g]
        o = jnp.where(lane_o < hd, acc[:tq], acc[tq:])
        cols = slice(g * LANES, (g + 1) * LANES)
        o_ref[:, cols] = (o * g_ref[:, cols].astype(F32)).astype(BF16)


def _sb_prompt(bias, q, kt, vt, gate, *, tq, gp):
    n, t, w = q.shape
    gw = gp * LANES
    assert w % gw == 0 and t % tq == 0 and tq % LANES == 0
    negu = _neg_suffix(tq)
    blk = pl.BlockSpec((None, tq, gw), lambda b, p, i: (b, i, p))
    seq = pl.BlockSpec((None, gw, t), lambda b, p, i: (b, p, 0))
    return pl.pallas_call(
        functools.partial(_sb_prompt_kernel, tq=tq, gp=gp),
        grid=(n, w // gw, t // tq),
        in_specs=[pl.BlockSpec(memory_space=pltpu.SMEM), blk, seq, seq, blk,
                  pl.BlockSpec(negu.shape, lambda b, p, i: (0, 0))],
        out_specs=blk,
        out_shape=jax.ShapeDtypeStruct((n, t, w), BF16),
        scratch_shapes=[pltpu.VMEM((gp, 2 * LANES, t), BF16), pltpu.VMEM((gp, 2 * tq, LANES), F32),
                        pltpu.VMEM((gp, 2 * tq, LANES), F32), pltpu.VMEM((2, gp, 2 * tq, tq), F32)],
        compiler_params=_cparams(("parallel", "parallel", "arbitrary")), name="sb_prompt",
    )(bias, q, kt, vt, gate, negu)


def _sb_paged_kernel(pt_ref, qbd_ref, kn_ref, vn_ref, g_ref, bias_ref, negu_ref, *rest, pp, heads, tnew):
    k_refs = rest[:pp]
    v_refs = rest[pp:2 * pp]
    o_ref, acc_ref, r_ref = rest[2 * pp:]
    g = pl.program_id(1)
    qbd = qbd_ref[...]
    bias = bias_ref[...]
    negu = negu_ref[...]
    rows, page = bias.shape

    @pl.when(g == 0)
    def _():
        z = _mm_nt(qbd, kn_ref[...]) + bias
        row = lax.broadcasted_iota(jnp.int32, (rows, page), 0)
        col = lax.broadcasted_iota(jnp.int32, (rows, page), 1)
        mask = col < (row % tnew)
        a, r = _sb_tile(z, jnp.zeros((rows, page), F32), negu[:page, :page], mask)
        acc_ref[...] = _mm_nt(vn_ref[...], a)
        r_ref[...] = r

    bias2 = jnp.concatenate([bias, bias], axis=1)
    pair = lambda refs, w: jnp.concatenate([refs[2 * w][...].astype(BF16), refs[2 * w + 1][...].astype(BF16)],
                                           axis=1)
    order = list(reversed(range(pp // 2)))
    zs = [jnp.dot(qbd, pair(k_refs, w), preferred_element_type=F32) + bias2 for w in order]
    sp_all = jnp.concatenate([_softplus2(z) for z in zs], axis=0).astype(BF16)
    incl_all = jnp.dot(sp_all, negu, preferred_element_type=F32)
    r = r_ref[...]
    acc = acc_ref[...]
    for n, w in enumerate(order):
        incl = incl_all[n * rows:(n + 1) * rows]
        a = jnp.exp2(zs[n] + incl + jnp.concatenate([r, r], axis=1)).astype(BF16)
        acc = acc + _mm_nt(pair(v_refs, w), a)
        r = r + jnp.broadcast_to(incl[:, 0:1], r.shape)
    acc_ref[...] = acc
    r_ref[...] = r

    @pl.when(g == pl.num_programs(1) - 1)
    def _():
        width = acc_ref.shape[0]
        eye = (lax.broadcasted_iota(jnp.int32, (rows, rows), 0)
               == lax.broadcasted_iota(jnp.int32, (rows, rows), 1)).astype(F32).astype(BF16)
        acc_t = sum(_mm_nt(eye, piece) for piece in _split3(acc_ref[...]))
        lane_h = lax.broadcasted_iota(jnp.int32, (tnew, width), 1) // SB_HEAD_DIM
        o = jnp.zeros((tnew, width), F32)
        for h in range(heads):
            o = o + jnp.where(lane_h == h, acc_t[h * tnew:(h + 1) * tnew], 0.0)
        o_ref[...] = (o * g_ref[...].astype(F32)).astype(BF16)


def _sb_paged(page_table, bias, q, k_new, v_new, gate, kpool, vpool, *, pp):
    n, tnew, w = q.shape
    heads = w // SB_HEAD_DIM
    npages = page_table.shape[1]
    page = kpool.shape[2]
    assert npages % pp == 0 and pp % 2 == 0 and tnew <= page and tnew % SUBLANES == 0
    rows = heads * tnew
    head_of_lane = jnp.arange(w) // SB_HEAD_DIM
    sel = (head_of_lane[None, :] == jnp.arange(heads)[:, None])
    qbd = jnp.where(sel[None, :, None, :], q[:, None, :, :], 0).reshape(n, rows, w)
    pad = ((0, 0), (0, page - tnew), (0, 0))
    kn = jnp.pad(k_new, pad)
    vn = jnp.swapaxes(jnp.pad(v_new, pad), 1, 2)
    bias_rep = jnp.broadcast_to(jnp.repeat(bias.astype(F32) * LOG2E, tnew)[:, None], (rows, page))
    negu = _neg_suffix(2 * page)

    def page_map(b, g, pt, *, r):
        return (pt[b, npages - (g + 1) * pp + r], 0, 0)

    per_seq = lambda shape: pl.BlockSpec((None,) + shape, lambda b, g, pt: (b, 0, 0))
    const = lambda a: pl.BlockSpec(a.shape, lambda b, g, pt: (0,) * a.ndim)
    page_specs = [pl.BlockSpec((None, w, page), functools.partial(page_map, r=r)) for r in range(pp)]
    grid_spec = pltpu.PrefetchScalarGridSpec(
        num_scalar_prefetch=1, grid=(n, npages // pp),
        in_specs=[per_seq((rows, w)), per_seq((page, w)), per_seq((w, page)), per_seq((tnew, w)),
                  const(bias_rep), const(negu)] + page_specs + page_specs,
        out_specs=per_seq((tnew, w)),
        scratch_shapes=[pltpu.VMEM((w, rows), F32), pltpu.VMEM((rows, page), F32)],
    )
    return pl.pallas_call(
        functools.partial(_sb_paged_kernel, pp=pp, heads=heads, tnew=tnew),
        grid_spec=grid_spec,
        out_shape=jax.ShapeDtypeStruct((n, tnew, w), BF16),
        compiler_params=_cparams(("parallel", "arbitrary")), name="sb_paged",
    )(page_table, qbd, kn, vn, gate, bias_rep, negu, *([kpool] * pp), *([vpool] * pp))


def _causal_conv_rows(x, prev8, w, width):
    rows = x.shape[0]
    xx = jnp.concatenate([prev8, x], axis=0)
    y = x * w[width - 1:width, :]
    for s in range(1, width):
        y = y + pltpu.roll(xx, s, 0)[SUBLANES:SUBLANES + rows] * w[width - 1 - s:width - s, :]
    return y


def _transpose_exact(x):
    r = lax.broadcasted_iota(jnp.int32, (LANES, LANES), 0)
    c = lax.broadcasted_iota(jnp.int32, (LANES, LANES), 1)
    eye = jnp.where(r == c, 1.0, 0.0).astype(BF16)
    hi, mid, lo = _split3(x)
    return _mm_nt(eye, hi) + (_mm_nt(eye, mid) + _mm_nt(eye, lo))


def _gdn_kernel(x_ref, tail_ref, gate_ref, prev_ref, s0_ref, cw_ref, gp_ref, onw_ref,
                o_ref, sout_ref, s_ref, carry_ref, *, chunk):
    b = pl.program_id(1)
    heads, dk = GDN_HEADS, GDN_DK
    sq, tb = x_ref.shape[0], x_ref.shape[1]
    nc = tb // chunk
    qkw = heads * dk

    @pl.when(b == 0)
    def _():
        s_ref[...] = s0_ref[...]
        carry_ref[...] = prev_ref[...]

    cw = cw_ref[...]
    gp = gp_ref[...]
    onw = onw_ref[...]
    ri = lax.broadcasted_iota(jnp.int32, (tb, tb), 0)
    ci = lax.broadcasted_iota(jnp.int32, (tb, tb), 1)
    same = (ri // chunk) == (ci // chunk)
    incl = same & (ri >= ci)
    strict = same & (ri > ci)
    eye = jnp.where(ri == ci, 1.0, 0.0)
    sum_lhs = jnp.concatenate([jnp.where(incl, 1.0, 0.0), jnp.where(same, 1.0, 0.0)], axis=0)

    beta_all, gc_all, gl_all, gc_t = [], [], [], []
    for i in range(sq):
        tl = tail_ref[i]
        beta_all.append(1.0 / (1.0 + jnp.exp(-tl)))
        xg = tl + gp[1:2, :]
        g_all = -jnp.exp(gp[0:1, :]) * (jnp.maximum(xg, 0.0) + jnp.log(1.0 + jnp.exp(-jnp.abs(xg))))
        sums = _mm_exact_lhs(sum_lhs, g_all)
        gc_all.append(sums[:tb])
        gl_all.append(sums[tb:])
        gc_t.append(_transpose_exact(sums[:tb]))

    chains = [(i, h) for i in range(sq) for h in range(heads)]
    each = lambda f: [f(n) for n in range(len(chains))]

    def conv_cols(i, col0):
        cols = slice(col0, col0 + dk)
        return _silu(_causal_conv_rows(x_ref[i, :, cols], carry_ref[i, :, cols], cw[:, cols], GDN_CONV))

    q = [conv_cols(i, h * dk) for i, h in chains]
    k = [conv_cols(i, qkw + h * dk) for i, h in chains]
    v = [conv_cols(i, 2 * qkw + h * dk) for i, h in chains]
    q = each(lambda n: q[n] * (lax.rsqrt(jnp.sum(q[n] * q[n], -1, keepdims=True) + NORM_EPS) * dk ** -0.5))
    k = each(lambda n: k[n] * lax.rsqrt(jnp.sum(k[n] * k[n], -1, keepdims=True) + NORM_EPS))
    beta = [beta_all[i][:, h:h + 1] for i, h in chains]
    gc = [gc_all[i][:, heads + h:heads + h + 1] for i, h in chains]
    gl = [gl_all[i][:, heads + h:heads + h + 1] for i, h in chains]
    decay = [jnp.exp(jnp.where(incl, gc[n] - jnp.broadcast_to(gc_t[i][heads + h:heads + h + 1, :], (tb, tb)), 0.0))
             for n, (i, h) in enumerate(chains)]
    kb = each(lambda n: k[n] * beta[n])
    kk = each(lambda n: _mm_nt(kb[n], k[n]))
    qk = each(lambda n: _mm_nt(q[n], k[n]))
    m = each(lambda n: jnp.where(strict, kk[n] * decay[n], 0.0))
    qk = each(lambda n: jnp.where(incl, qk[n] * decay[n], 0.0))
    egc = each(lambda n: jnp.exp(gc[n]))
    rhs = each(lambda n: jnp.concatenate([v[n] * beta[n], kb[n] * egc[n]], axis=1))
    inv = each(lambda n: eye - m[n])
    p = m
    for _ in range(max(0, int(math.log2(chunk)) - 1)):
        p = each(lambda n: _mm(p[n], p[n]))
        inv = each(lambda n: inv[n] + _mm(inv[n], p[n]))
    uw = each(lambda n: _mm(inv[n], rhs[n]))
    res = each(lambda n: rhs[n] - uw[n] - _mm3(m[n], uw[n]))
    uw = each(lambda n: uw[n] + _mm(inv[n], res[n]))
    qe = each(lambda n: q[n] * egc[n])
    k_dec = each(lambda n: k[n] * jnp.exp(gl[n] - gc[n]))
    s_dec = each(lambda n: jnp.exp(gl[n]))

    s = [s_ref[i, h] for i, h in chains]
    v_parts = [[] for _ in chains]
    o_parts = [[] for _ in chains]
    for c in range(nc):
        rows = slice(c * chunk, (c + 1) * chunk)
        v_new = each(lambda n: uw[n][rows, :dk] - _mm(uw[n][rows, dk:], s[n]))
        for n in range(len(chains)):
            v_parts[n].append(v_new[n])
        pad = [jnp.zeros(((nc - 1 - c) * chunk, dk), F32)] * (c < nc - 1)
        o_c = each(lambda n: _mm(qe[n][rows], s[n]) + _mm(qk[n][rows], jnp.concatenate(v_parts[n] + pad, axis=0)))
        for n in range(len(chains)):
            o_parts[n].append(o_c[n])
        s = each(lambda n: s[n] * s_dec[n][c * chunk:c * chunk + 1, :] + lax.dot_general(
            k_dec[n][rows].astype(BF16), v_new[n].astype(BF16), (((0,), (0,)), ((), ())),
            preferred_element_type=F32))
    for n, (i, h) in enumerate(chains):
        s_ref[i, h] = s[n]
        o = jnp.concatenate(o_parts[n], axis=0)
        on = o * lax.rsqrt(jnp.mean(o * o, -1, keepdims=True) + NORM_EPS) * onw
        gate = gate_ref[i, :, h * dk:(h + 1) * dk].astype(F32)
        o_ref[i, :, h * dk:(h + 1) * dk] = (on * gate).astype(BF16)
    carry_ref[...] = x_ref[:, tb - SUBLANES:, :]

    @pl.when(b == pl.num_programs(1) - 1)
    def _():
        sout_ref[...] = s_ref[...]


def _gdn(gqkv, tail, gate, prev8, s0, conv_w, gparams, out_norm_w, *, tb, chunk, sq):
    n, t, w = gqkv.shape
    heads, dk = GDN_HEADS, GDN_DK
    assert t % tb == 0 and tb % chunk == 0 and tb % SUBLANES == 0 and n % sq == 0
    tok = lambda width: pl.BlockSpec((sq, tb, width), lambda i, b: (i, b, 0))
    const = lambda a: pl.BlockSpec(a.shape, lambda i, b: (0,) * a.ndim)
    state = pl.BlockSpec((sq, heads, dk, dk), lambda i, b: (i, 0, 0, 0))
    return pl.pallas_call(
        functools.partial(_gdn_kernel, chunk=chunk),
        grid=(n // sq, t // tb),
        in_specs=[tok(w), tok(LANES), tok(heads * dk),
                  pl.BlockSpec((sq, SUBLANES, w), lambda i, b: (i, 0, 0)), state,
                  const(conv_w), const(gparams), const(out_norm_w)],
        out_specs=(tok(heads * dk), state),
        out_shape=(jax.ShapeDtypeStruct((n, t, heads * dk), BF16),
                   jax.ShapeDtypeStruct((n, heads, dk, dk), F32)),
        scratch_shapes=[pltpu.VMEM((sq, heads, dk, dk), F32), pltpu.VMEM((sq, SUBLANES, w), F32)],
        compiler_params=_cparams(("parallel", "arbitrary")), name="gdn",
    )(gqkv, tail, gate, prev8, s0, conv_w, gparams, out_norm_w)


def _tail_kernel(x_ref, ma_ref, mb_ref, prev_ref, wo_ref, nw_ref, win_ref, cw_ref, wout_ref, fnw_ref,
                 y_ref, last_ref, carry_ref, *, tt):
    b = pl.program_id(1)
    rows, d = x_ref.shape
    sa = ma_ref.shape[1]

    @pl.when(b == 0)
    def _():
        carry_ref[...] = prev_ref[...]

    h1 = x_ref[...] + (jnp.dot(ma_ref[...], wo_ref[:sa, :], preferred_element_type=F32)
                       + jnp.dot(mb_ref[...], wo_ref[sa:, :], preferred_element_type=F32))
    xn = _rms(h1, nw_ref[...]).astype(BF16)
    proj = lambda i: jnp.dot(xn, win_ref[:, i * d:(i + 1) * d], preferred_element_type=F32)
    cu = proj(2) * proj(0)
    cw = cw_ref[...]
    if tt == rows:
        yc = _causal_conv_rows(cu, carry_ref[...], cw, SC_CONV)
        carry_ref[...] = cu[rows - SUBLANES:, :]
        last_ref[...] = cu[rows - SUBLANES:, :]
    else:
        assert tt == SUBLANES
        prev = carry_ref[...]
        tok = lax.broadcasted_iota(jnp.int32, (rows, d), 0) % tt
        yc = cu * cw[SC_CONV - 1:SC_CONV, :]
        for s in range(1, SC_CONV):
            shifted = jnp.where(tok >= s, pltpu.roll(cu, s, 0), pltpu.roll(prev, (s - tt) % rows, 0))
            yc = yc + shifted * cw[SC_CONV - 1 - s:SC_CONV - s, :]
        last_ref[...] = cu
    m = (proj(1) * yc * _silu(proj(3))).astype(BF16)
    h2 = h1 + jnp.dot(m, wout_ref[...], preferred_element_type=F32)
    y_ref[...] = _rms(h2, fnw_ref[...])


def _tail(x, mix_a, mix_b, prev, w_out0, norm_w1, sc_w_in, sc_conv_w, sc_w_out, final_norm_w, *, tm, tt):
    g, r, d = x.shape
    assert r % tm == 0 and tm % tt == 0 and (tt == tm or (tt == SUBLANES and r == tm))
    nseq = tm // tt
    tok = lambda width: pl.BlockSpec((None, tm, width), lambda i, b: (i, b, 0))
    const = lambda a: pl.BlockSpec(a.shape, lambda i, b: (0,) * a.ndim, pipeline_mode=pl.Buffered(1))
    small = pl.BlockSpec((None, nseq * SUBLANES, d), lambda i, b: (i, 0, 0))
    return pl.pallas_call(
        functools.partial(_tail_kernel, tt=tt),
        grid=(g, r // tm),
        in_specs=[tok(d), tok(mix_a.shape[2]), tok(mix_b.shape[2]), small,
                  const(w_out0), const(norm_w1), const(sc_w_in), const(sc_conv_w), const(sc_w_out),
                  const(final_norm_w)],
        out_specs=(tok(d), small),
        out_shape=(jax.ShapeDtypeStruct((g, r, d), F32),
                   jax.ShapeDtypeStruct((g, nseq * SUBLANES, d), F32)),
        scratch_shapes=[pltpu.VMEM((nseq * SUBLANES, d), F32)],
        compiler_params=_cparams(("parallel", "arbitrary")), name="tail",
    )(x, mix_a, mix_b, prev, w_out0, norm_w1, sc_w_in, sc_conv_w, sc_w_out, final_norm_w)


def _pad_front_rows(buf, rows):
    return jnp.pad(buf, ((0, 0), (rows - buf.shape[1], 0), (0, 0)))


def _trunk(x, weights, sb_fn, gdn_state, gdn_buf, sc_buf, *, kv_transposed, tm_proj, gdn_tb, gdn_chunk, gdn_sq,
           tail_tm, tail_tt):
    (norm_w, w_main, w_kvt, w_tail, conv_w, gparams, out_norm_w, w_out0, sc_w_in, sc_conv_w, sc_w_out,
     final_norm_w) = weights
    n, t, d = x.shape
    sbw = w_out0.shape[0] // 2
    heads = sbw // SB_HEAD_DIM
    gvw = GDN_HEADS * GDN_DK
    gqkvw = conv_w.shape[1]
    xp = x if kv_transposed else x.reshape(1, n * t, d)
    q, kb, vb, kf, vf, sg, gqkv, gg, tail = _proj0(
        xp, norm_w[0:1], w_main, w_kvt, w_tail, sbw=sbw, gqkvw=gqkvw, gvw=gvw, tm=tm_proj,
        kv_transposed=kv_transposed)
    r3 = lambda a: a.reshape(n, t, a.shape[-1])
    if kv_transposed:
        new_kv = lambda a: jnp.transpose(a.reshape(n, heads, SB_HEAD_DIM, t), (0, 3, 1, 2))[None]
        mix_a = sb_fn(q, kb, vb, sg)
    else:
        new_kv = lambda a: a.reshape(1, n, t, heads, SB_HEAD_DIM)
        mix_a = sb_fn(r3(q), r3(kb), r3(vb), r3(sg))
    new_k, new_v = new_kv(kf), new_kv(vf)
    gqkv3 = r3(gqkv)
    mix_b, new_state = _gdn(gqkv3, r3(tail), r3(gg), _pad_front_rows(gdn_buf, SUBLANES), gdn_state,
                            conv_w, gparams, out_norm_w, tb=gdn_tb, chunk=gdn_chunk, sq=gdn_sq)
    prev_sc = _pad_front_rows(sc_buf, SUBLANES)
    if tail_tt == tail_tm:
        groups = (n, t)
    else:
        groups = (1, n * t)
        prev_sc = prev_sc.reshape(1, n * SUBLANES, d)
    shp = lambda a: a.reshape(groups + (a.shape[-1],))
    y, last = _tail(shp(x), shp(mix_a), shp(mix_b), prev_sc, w_out0, norm_w[1:2], sc_w_in, sc_conv_w,
                    sc_w_out, final_norm_w, tm=tail_tm, tt=tail_tt)
    new_gbuf = gqkv3[:, t - (GDN_CONV - 1):, :][None]
    new_scbuf = last.reshape(n, SUBLANES, d)[:, SUBLANES - (SC_CONV - 1):, :][None]
    return y.reshape(n, t, d), new_k, new_v, new_state[None], new_gbuf, new_scbuf


def kernel(x_prompt, x_sample, cache_sb_k, cache_sb_v, state_gdn, state_gdn_conv, state_sconv, page_table,
           norm_w, ab_w_in, ab_sb_bias, ab_conv_w, ab_a_log, ab_dt_bias, ab_out_norm_w, ab_w_out,
           sc_w_in, sc_conv_w, sc_w_out, final_norm_w):
    assert ab_w_in.shape[0] == 1 and sc_w_in.shape[0] == 1, "one layer of each kind"
    nb, t, d = x_prompt.shape
    ns, ts, _ = x_sample.shape
    heads = GDN_HEADS
    main_cols = ab_w_in.shape[2] - 2 * heads
    sbw = ab_w_out.shape[1] // 2
    w_main = ab_w_in[0, :, :main_cols].astype(BF16)
    w_kvt = ab_w_in[0, :, sbw:3 * sbw].T.astype(BF16)
    w_tail = jnp.pad(ab_w_in[0, :, main_cols:], ((0, 0), (0, LANES - 2 * heads))).astype(BF16)
    gparams = jnp.zeros((SUBLANES, LANES), F32)
    gparams = gparams.at[0, heads:2 * heads].set(ab_a_log[0]).at[1, heads:2 * heads].set(ab_dt_bias[0])
    weights = (norm_w, w_main, w_kvt, w_tail, ab_conv_w[0], gparams, ab_out_norm_w[0:1],
               ab_w_out[0].astype(BF16), sc_w_in[0].astype(BF16), sc_conv_w[0], sc_w_out[0].astype(BF16),
               final_norm_w[None])
    bias = ab_sb_bias[0]

    zeros = lambda *s: jnp.zeros(s, F32)
    tm = min(512, t)
    y_p, k_p, v_p, s_p, gb_p, sc_p = _trunk(
        x_prompt, weights, functools.partial(_sb_prompt, bias, tq=min(256, t), gp=4),
        zeros(nb, heads, GDN_DK, GDN_DK), zeros(nb, GDN_CONV - 1, ab_conv_w.shape[2]),
        zeros(nb, SC_CONV - 1, d),
        kv_transposed=True, tm_proj=tm, gdn_tb=min(256, t), gdn_chunk=min(64, t), gdn_sq=4, tail_tm=tm, tail_tt=tm)

    n_pool, page = cache_sb_k.shape[1], cache_sb_k.shape[2]
    pool_t = lambda c: jnp.transpose(c, (0, 1, 3, 4, 2)).reshape(n_pool, -1, page)
    kpool = pool_t(cache_sb_k)
    vpool = pool_t(cache_sb_v)
    sb_sample = lambda q, kb, vb, sg: _sb_paged(page_table, bias, q, kb, vb, sg, kpool, vpool,
                                                pp=min(16, page_table.shape[1]))
    y_s, k_s, v_s, s_s, gb_s, sc_s = _trunk(
        x_sample, weights, sb_sample, state_gdn[0], state_gdn_conv[0], state_sconv[0],
        kv_transposed=False, tm_proj=ns * ts, gdn_tb=ts, gdn_chunk=ts, gdn_sq=2, tail_tm=ns * ts, tail_tt=ts)
    return (y_p, y_s, k_p, v_p, k_s, v_s, s_p, s_s, gb_p, gb_s, sc_p, sc_s)
```

```python
import functools
import math

import jax
import jax.numpy as jnp
from jax import lax
from jax.experimental import pallas as pl
from jax.experimental.pallas import tpu as pltpu

F32 = jnp.float32
BF16 = jnp.bfloat16
LOG2E = 1.4426950408889634
NORM_EPS = 1e-6
LANES = 128
SUBLANES = 8
VMEM_LIMIT_BYTES = 56 * 1024 * 1024

SB_HEAD_DIM = 64
GDN_HEADS = 4
GDN_DK = 128
GDN_CONV = 4
SC_CONV = 3


def _cparams(sem):
    return pltpu.CompilerParams(dimension_semantics=sem, vmem_limit_bytes=VMEM_LIMIT_BYTES)


def _silu(x):
    return x * (1.0 / (1.0 + jnp.exp(-x)))


def _rms(x, w):
    return x * lax.rsqrt(jnp.mean(x * x, axis=-1, keepdims=True) + NORM_EPS) * w


def _mm(a, b):
    return jnp.dot(a.astype(BF16), b.astype(BF16), preferred_element_type=F32)


def _mm_nt(a, b):
    return lax.dot_general(a.astype(BF16), b.astype(BF16), (((1,), (1,)), ((), ())),
                           preferred_element_type=F32)


def _split3(x):
    hi = x.astype(BF16)
    r = x - hi.astype(F32)
    mid = r.astype(BF16)
    lo = (r - mid.astype(F32)).astype(BF16)
    return hi, mid, lo


def _mm3(a, b):
    ah, am, _ = _split3(a)
    bh, bm, _ = _split3(b)
    d = functools.partial(jnp.dot, preferred_element_type=F32)
    return d(ah, bh) + (d(ah, bm) + d(am, bh))


def _mm_exact_lhs(a01, b):
    a = a01.astype(BF16)
    bh, bm, bl = _split3(b)
    d = functools.partial(jnp.dot, preferred_element_type=F32)
    return d(a, bh) + (d(a, bm) + d(a, bl))


def _softplus2(z):
    neg_abs = pltpu.bitcast(pltpu.bitcast(z, jnp.uint32) | jnp.uint32(0x80000000), F32)
    return jnp.maximum(z, 0.0) + jnp.log2(1.0 + jnp.exp2(neg_abs))


def _proj0_kernel(x_ref, nw_ref, w_ref, wkvt_ref, wt_ref,
                  q_ref, kb_ref, vb_ref, kf_ref, vf_ref, sg_ref, gqkv_ref, gg_ref, tail_ref,
                  *, sbw, gqkvw, gvw, qscale, kv_transposed):
    x = x_ref[...]
    xn = _rms(x, nw_ref[...]).astype(BF16)

    def proj(lo, width):
        return jnp.dot(xn, w_ref[:, lo:lo + width], preferred_element_type=F32)

    q_ref[...] = (proj(0, sbw) * qscale).astype(BF16)
    if kv_transposed:
        nt = lambda wt: lax.dot_general(wt, xn, (((1,), (1,)), ((), ())), preferred_element_type=F32)
        k = nt(wkvt_ref[:sbw, :])
        v = nt(wkvt_ref[sbw:, :])
    else:
        k = proj(sbw, sbw)
        v = proj(2 * sbw, sbw)
    kf_ref[...] = k
    kb_ref[...] = k.astype(BF16)
    vf_ref[...] = v
    vb_ref[...] = v.astype(BF16)
    sg_ref[...] = _silu(proj(3 * sbw, sbw)).astype(BF16)
    gqkv_ref[...] = proj(4 * sbw, gqkvw)
    gg_ref[...] = _silu(proj(4 * sbw + gqkvw, gvw)).astype(BF16)
    tail_ref[...] = jnp.dot(xn, wt_ref[...], preferred_element_type=F32)


def _proj0(x, norm_w, w_main, w_kvt, w_tail, *, sbw, gqkvw, gvw, tm, kv_transposed):
    g, rows, d = x.shape
    assert rows % tm == 0
    qscale = LOG2E * SB_HEAD_DIM ** -0.5
    kern = functools.partial(_proj0_kernel, sbw=sbw, gqkvw=gqkvw, gvw=gvw, qscale=qscale,
                             kv_transposed=kv_transposed)
    row = lambda w: pl.BlockSpec((None, tm, w), lambda b, i: (b, i, 0))
    full = lambda a: pl.BlockSpec(a.shape, lambda b, i: (0,) * a.ndim, pipeline_mode=pl.Buffered(1))
    rs = lambda w, dt: jax.ShapeDtypeStruct((g, rows, w), dt)
    if kv_transposed:
        kv_spec = pl.BlockSpec((None, sbw, tm), lambda b, i: (b, 0, i))
        kv_shape = lambda dt: jax.ShapeDtypeStruct((g, sbw, rows), dt)
    else:
        kv_spec = row(sbw)
        kv_shape = lambda dt: rs(sbw, dt)
    out_shapes = (
        rs(sbw, BF16),
        kv_shape(BF16),
        kv_shape(BF16),
        kv_shape(F32),
        kv_shape(F32),
        rs(sbw, BF16),
        rs(gqkvw, F32),
        rs(gvw, BF16),
        rs(LANES, F32),
    )
    out_specs = (row(sbw), kv_spec, kv_spec, kv_spec, kv_spec, row(sbw), row(gqkvw), row(gvw), row(LANES))
    return pl.pallas_call(
        kern, grid=(g, rows // tm),
        in_specs=[row(d), full(norm_w), full(w_main), full(w_kvt), full(w_tail)],
        out_specs=out_specs, out_shape=out_shapes,
        compiler_params=_cparams(("parallel", "parallel")), name="proj0",
    )(x, norm_w, w_main, w_kvt, w_tail)


def _neg_suffix(tk):
    j = lax.broadcasted_iota(jnp.int32, (tk, tk), 0)
    s = lax.broadcasted_iota(jnp.int32, (tk, tk), 1)
    return jnp.where(j >= s, -1.0, 0.0).astype(BF16)


def _sb_tile(z, r_rep, negu, mask):
    tk = z.shape[1]
    sp = _softplus2(z)
    if mask is not None:
        sp = jnp.where(mask, sp, 0.0)
    incl = jnp.dot(sp.astype(BF16), negu, preferred_element_type=F32)
    r_all = r_rep if tk == LANES else jnp.concatenate([r_rep] * (tk // LANES), axis=1)
    a = jnp.exp2(z + incl + r_all)
    if mask is not None:
        a = jnp.where(mask, a, 0.0)
    return a.astype(BF16), r_rep + jnp.broadcast_to(incl[:, 0:1], r_rep.shape)


BIAS_PIECES = 3


def _sb_prompt_kernel(bias_ref, q_ref, k_ref, v_ref, g_ref, negu_ref, o_ref, kaug_ref, acc_ref, r_ref, z_ref,
                      *, tq, gp):
    pg = pl.program_id(1)
    i = pl.program_id(2)
    hd = SB_HEAD_DIM

    @pl.when(i == 0)
    def _():
        rown = lax.broadcasted_iota(jnp.int32, (LANES, k_ref.shape[1]), 0)
        ones_rows = jnp.where(rown < BIAS_PIECES, 1.0, 0.0).astype(BF16)
        for g in range(gp):
            kaug_ref[g, :LANES, :] = k_ref[g * LANES:(g + 1) * LANES, :]
            kaug_ref[g, LANES:, :] = ones_rows

    lane = lax.broadcasted_iota(jnp.int32, (tq, LANES), 1)
    zero = jnp.zeros((tq, LANES), BF16)

    def bias_lanes(b):
        pieces = _split3(jnp.full((tq, LANES), b * LOG2E, F32))
        out = jnp.zeros((tq, LANES), F32)
        for n, piece in enumerate(pieces):
            out = jnp.where(lane == n, piece.astype(F32), out)
        return out.astype(BF16)

    def stacked_q(g):
        q = q_ref[:, g * LANES:(g + 1) * LANES]
        h0 = 2 * (pg * gp + g)
        return jnp.concatenate(
            [jnp.concatenate([jnp.where(lane < hd, q, zero), bias_lanes(bias_ref[h0])], axis=1),
             jnp.concatenate([jnp.where(lane >= hd, q, zero), bias_lanes(bias_ref[h0 + 1])], axis=1)],
            axis=0)

    q2 = [stacked_q(g) for g in range(gp)]
    negu = negu_ref[...]

    def span(j):
        return pl.ds(pl.multiple_of(j * tq, tq), tq)

    def logits(g, j):
        return jnp.dot(q2[g], kaug_ref[g, :, span(j)], preferred_element_type=F32)

    def attend(g, j, a):
        return _mm_nt(a, v_ref[g * LANES:(g + 1) * LANES, span(j)])

    for g in range(gp):
        z_ref[0, g] = logits(g, jnp.maximum(i - 1, 0))

    row = lax.broadcasted_iota(jnp.int32, (2 * tq, tq), 0)
    col = lax.broadcasted_iota(jnp.int32, (2 * tq, tq), 1)
    diag_mask = col < jnp.where(row >= tq, row - tq, row)
    zs = [logits(g, i) for g in range(gp)]
    for g in range(gp):
        a, r = _sb_tile(zs[g], jnp.zeros((2 * tq, LANES), F32), negu, diag_mask)
        acc_ref[g] = attend(g, i, a)
        r_ref[g] = r

    def step(j, slot, prefetch=True):
        for g in range(gp if prefetch else 0):
            z_ref[1 - slot, g] = logits(g, jnp.maximum(j - 1, 0))
        for g in range(gp):
            a, r = _sb_tile(z_ref[slot, g], r_ref[g], negu, None)
            acc_ref[g] += attend(g, j, a)
            r_ref[g] = r

    def body(jj, carry):
        j = i - 1 - 2 * jj
        step(j, 0)
        step(j - 1, 1)
        return carry

    lax.fori_loop(0, i // 2, body, 0)

    @pl.when(i % 2 == 1)
    def _():
        step(0, 0, prefetch=False)
    lane_o = lax.broadcasted_iota(jnp.int32, (tq, 2 * hd), 1)
    for g in range(gp):
        acc = acc_ref[g]
        o = jnp.where(lane_o < hd, acc[:tq], acc[tq:])
        cols = slice(g * LANES, (g + 1) * LANES)
        o_ref[:, cols] = (o * g_ref[:, cols].astype(F32)).astype(BF16)


def _sb_prompt(bias, q, kt, vt, gate, *, tq, gp):
    n, t, w = q.shape
    gw = gp * LANES
    assert w % gw == 0 and t % tq == 0 and tq % LANES == 0
    negu = _neg_suffix(tq)
    blk = pl.BlockSpec((None, tq, gw), lambda b, p, i: (b, i, p))
    seq = pl.BlockSpec((None, gw, t), lambda b, p, i: (b, p, 0))
    return pl.pallas_call(
        functools.partial(_sb_prompt_kernel, tq=tq, gp=gp),
        grid=(n, w // gw, t // tq),
        in_specs=[pl.BlockSpec(memory_space=pltpu.SMEM), blk, seq, seq, blk,
                  pl.BlockSpec(negu.shape, lambda b, p, i: (0, 0))],
        out_specs=blk,
        out_shape=jax.ShapeDtypeStruct((n, t, w), BF16),
        scratch_shapes=[pltpu.VMEM((gp, 2 * LANES, t), BF16), pltpu.VMEM((gp, 2 * tq, LANES), F32),
                        pltpu.VMEM((gp, 2 * tq, LANES), F32), pltpu.VMEM((2, gp, 2 * tq, tq), F32)],
        compiler_params=_cparams(("parallel", "parallel", "arbitrary")), name="sb_prompt",
    )(bias, q, kt, vt, gate, negu)


def _sb_paged_kernel(pt_ref, qbd_ref, kn_ref, vn_ref, g_ref, bias_ref, negu_ref, *rest, pp, heads, tnew):
    k_refs = rest[:pp]
    v_refs = rest[pp:2 * pp]
    o_ref, acc_ref, r_ref = rest[2 * pp:]
    g = pl.program_id(1)
    qbd = qbd_ref[...]
    bias = bias_ref[...]
    negu = negu_ref[...]
    rows, page = bias.shape

    @pl.when(g == 0)
    def _():
        z = _mm_nt(qbd, kn_ref[...]) + bias
        row = lax.broadcasted_iota(jnp.int32, (rows, page), 0)
        col = lax.broadcasted_iota(jnp.int32, (rows, page), 1)
        mask = col < (row % tnew)
        a, r = _sb_tile(z, jnp.zeros((rows, page), F32), negu[:page, :page], mask)
        acc_ref[...] = _mm_nt(vn_ref[...], a)
        r_ref[...] = r

    bias2 = jnp.concatenate([bias, bias], axis=1)
    pair = lambda refs, w: jnp.concatenate([refs[2 * w][...].astype(BF16), refs[2 * w + 1][...].astype(BF16)],
                                           axis=1)
    order = list(reversed(range(pp // 2)))
    zs = [jnp.dot(qbd, pair(k_refs, w), preferred_element_type=F32) + bias2 for w in order]
    sp_all = jnp.concatenate([_softplus2(z) for z in zs], axis=0).astype(BF16)
    incl_all = jnp.dot(sp_all, negu, preferred_element_type=F32)
    r = r_ref[...]
    acc = acc_ref[...]
    for n, w in enumerate(order):
        incl = incl_all[n * rows:(n + 1) * rows]
        a = jnp.exp2(zs[n] + incl + jnp.concatenate([r, r], axis=1)).astype(BF16)
        acc = acc + _mm_nt(pair(v_refs, w), a)
        r = r + jnp.broadcast_to(incl[:, 0:1], r.shape)
    acc_ref[...] = acc
    r_ref[...] = r

    @pl.when(g == pl.num_programs(1) - 1)
    def _():
        width = acc_ref.shape[0]
        eye = (lax.broadcasted_iota(jnp.int32, (rows, rows), 0)
               == lax.broadcasted_iota(jnp.int32, (rows, rows), 1)).astype(F32).astype(BF16)
        acc_t = sum(_mm_nt(eye, piece) for piece in _split3(acc_ref[...]))
        lane_h = lax.broadcasted_iota(jnp.int32, (tnew, width), 1) // SB_HEAD_DIM
        o = jnp.zeros((tnew, width), F32)
        for h in range(heads):
            o = o + jnp.where(lane_h == h, acc_t[h * tnew:(h + 1) * tnew], 0.0)
        o_ref[...] = (o * g_ref[...].astype(F32)).astype(BF16)


def _sb_paged(page_table, bias, q, k_new, v_new, gate, kpool, vpool, *, pp):
    n, tnew, w = q.shape
    heads = w // SB_HEAD_DIM
    npages = page_table.shape[1]
    page = kpool.shape[2]
    assert npages % pp == 0 and pp % 2 == 0 and tnew <= page and tnew % SUBLANES == 0
    rows = heads * tnew
    head_of_lane = jnp.arange(w) // SB_HEAD_DIM
    sel = (head_of_lane[None, :] == jnp.arange(heads)[:, None])
    qbd = jnp.where(sel[None, :, None, :], q[:, None, :, :], 0).reshape(n, rows, w)
    pad = ((0, 0), (0, page - tnew), (0, 0))
    kn = jnp.pad(k_new, pad)
    vn = jnp.swapaxes(jnp.pad(v_new, pad), 1, 2)
    bias_rep = jnp.broadcast_to(jnp.repeat(bias.astype(F32) * LOG2E, tnew)[:, None], (rows, page))
    negu = _neg_suffix(2 * page)

    def page_map(b, g, pt, *, r):
        return (pt[b, npages - (g + 1) * pp + r], 0, 0)

    per_seq = lambda shape: pl.BlockSpec((None,) + shape, lambda b, g, pt: (b, 0, 0))
    const = lambda a: pl.BlockSpec(a.shape, lambda b, g, pt: (0,) * a.ndim)
    page_specs = [pl.BlockSpec((None, w, page), functools.partial(page_map, r=r)) for r in range(pp)]
    grid_spec = pltpu.PrefetchScalarGridSpec(
        num_scalar_prefetch=1, grid=(n, npages // pp),
        in_specs=[per_seq((rows, w)), per_seq((page, w)), per_seq((w, page)), per_seq((tnew, w)),
                  const(bias_rep), const(negu)] + page_specs + page_specs,
        out_specs=per_seq((tnew, w)),
        scratch_shapes=[pltpu.VMEM((w, rows), F32), pltpu.VMEM((rows, page), F32)],
    )
    return pl.pallas_call(
        functools.partial(_sb_paged_kernel, pp=pp, heads=heads, tnew=tnew),
        grid_spec=grid_spec,
        out_shape=jax.ShapeDtypeStruct((n, tnew, w), BF16),
        compiler_params=_cparams(("parallel", "arbitrary")), name="sb_paged",
    )(page_table, qbd, kn, vn, gate, bias_rep, negu, *([kpool] * pp), *([vpool] * pp))


def _causal_conv_rows(x, prev8, w, width):
    rows = x.shape[0]
    xx = jnp.concatenate([prev8, x], axis=0)
    y = x * w[width - 1:width, :]
    for s in range(1, width):
        y = y + pltpu.roll(xx, s, 0)[SUBLANES:SUBLANES + rows] * w[width - 1 - s:width - s, :]
    return y


def _transpose_exact(x):
    r = lax.broadcasted_iota(jnp.int32, (LANES, LANES), 0)
    c = lax.broadcasted_iota(jnp.int32, (LANES, LANES), 1)
    eye = jnp.where(r == c, 1.0, 0.0).astype(BF16)
    hi, mid, lo = _split3(x)
    return _mm_nt(eye, hi) + (_mm_nt(eye, mid) + _mm_nt(eye, lo))


def _gdn_kernel(x_ref, tail_ref, gate_ref, prev_ref, s0_ref, cw_ref, gp_ref, onw_ref,
                o_ref, sout_ref, s_ref, carry_ref, *, chunk):
    b = pl.program_id(1)
    heads, dk = GDN_HEADS, GDN_DK
    sq, tb = x_ref.shape[0], x_ref.shape[1]
    nc = tb // chunk
    qkw = heads * dk

    @pl.when(b == 0)
    def _():
        s_ref[...] = s0_ref[...]
        carry_ref[...] = prev_ref[...]

    cw = cw_ref[...]
    gp = gp_ref[...]
    onw = onw_ref[...]
    ri = lax.broadcasted_iota(jnp.int32, (tb, tb), 0)
    ci = lax.broadcasted_iota(jnp.int32, (tb, tb), 1)
    same = (ri // chunk) == (ci // chunk)
    incl = same & (ri >= ci)
    strict = same & (ri > ci)
    eye = jnp.where(ri == ci, 1.0, 0.0)
    sum_lhs = jnp.concatenate([jnp.where(incl, 1.0, 0.0), jnp.where(same, 1.0, 0.0)], axis=0)

    beta_all, gc_all, gl_all, gc_t = [], [], [], []
    for i in range(sq):
        tl = tail_ref[i]
        beta_all.append(1.0 / (1.0 + jnp.exp(-tl)))
        xg = tl + gp[1:2, :]
        g_all = -jnp.exp(gp[0:1, :]) * (jnp.maximum(xg, 0.0) + jnp.log(1.0 + jnp.exp(-jnp.abs(xg))))
        sums = _mm_exact_lhs(sum_lhs, g_all)
        gc_all.append(sums[:tb])
        gl_all.append(sums[tb:])
        gc_t.append(_transpose_exact(sums[:tb]))

    chains = [(i, h) for i in range(sq) for h in range(heads)]
    each = lambda f: [f(n) for n in range(len(chains))]

    def conv_cols(i, col0):
        cols = slice(col0, col0 + dk)
        return _silu(_causal_conv_rows(x_ref[i, :, cols], carry_ref[i, :, cols], cw[:, cols], GDN_CONV))

    q = [conv_cols(i, h * dk) for i, h in chains]
    k = [conv_cols(i, qkw + h * dk) for i, h in chains]
    v = [conv_cols(i, 2 * qkw + h * dk) for i, h in chains]
    q = each(lambda n: q[n] * (lax.rsqrt(jnp.sum(q[n] * q[n], -1, keepdims=True) + NORM_EPS) * dk ** -0.5))
    k = each(lambda n: k[n] * lax.rsqrt(jnp.sum(k[n] * k[n], -1, keepdims=True) + NORM_EPS))
    beta = [beta_all[i][:, h:h + 1] for i, h in chains]
    gc = [gc_all[i][:, heads + h:heads + h + 1] for i, h in chains]
    gl = [gl_all[i][:, heads + h:heads + h + 1] for i, h in chains]
    decay = [jnp.exp(jnp.where(incl, gc[n] - jnp.broadcast_to(gc_t[i][heads + h:heads + h + 1, :], (tb, tb)), 0.0))
             for n, (i, h) in enumerate(chains)]
    kb = each(lambda n: k[n] * beta[n])
    kk = each(lambda n: _mm_nt(kb[n], k[n]))
    qk = each(lambda n: _mm_nt(q[n], k[n]))
    m = each(lambda n: jnp.where(strict, kk[n] * decay[n], 0.0))
    qk = each(lambda n: jnp.where(incl, qk[n] * decay[n], 0.0))
    egc = each(lambda n: jnp.exp(gc[n]))
    rhs = each(lambda n: jnp.concatenate([v[n] * beta[n], kb[n] * egc[n]], axis=1))
    inv = each(lambda n: eye - m[n])
    p = m
    for _ in range(max(0, int(math.log2(chunk)) - 1)):
        p = each(lambda n: _mm(p[n], p[n]))
        inv = each(lambda n: inv[n] + _mm(inv[n], p[n]))
    uw = each(lambda n: _mm(inv[n], rhs[n]))
    res = each(lambda n: rhs[n] - uw[n] - _mm3(m[n], uw[n]))
    uw = each(lambda n: uw[n] + _mm(inv[n], res[n]))
    qe = each(lambda n: q[n] * egc[n])
    k_dec = each(lambda n: k[n] * jnp.exp(gl[n] - gc[n]))
    s_dec = each(lambda n: jnp.exp(gl[n]))

    s = [s_ref[i, h] for i, h in chains]
    v_parts = [[] for _ in chains]
    o_parts = [[] for _ in chains]
    for c in range(nc):
        rows = slice(c * chunk, (c + 1) * chunk)
        v_new = each(lambda n: uw[n][rows, :dk] - _mm(uw[n][rows, dk:], s[n]))
        for n in range(len(chains)):
            v_parts[n].append(v_new[n])
        pad = [jnp.zeros(((nc - 1 - c) * chunk, dk), F32)] * (c < nc - 1)
        o_c = each(lambda n: _mm(qe[n][rows], s[n]) + _mm(qk[n][rows], jnp.concatenate(v_parts[n] + pad, axis=0)))
        for n in range(len(chains)):
            o_parts[n].append(o_c[n])
        s = each(lambda n: s[n] * s_dec[n][c * chunk:c * chunk + 1, :] + lax.dot_general(
            k_dec[n][rows].astype(BF16), v_new[n].astype(BF16), (((0,), (0,)), ((), ())),
            preferred_element_type=F32))
    for n, (i, h) in enumerate(chains):
        s_ref[i, h] = s[n]
        o = jnp.concatenate(o_parts[n], axis=0)
        on = o * lax.rsqrt(jnp.mean(o * o, -1, keepdims=True) + NORM_EPS) * onw
        gate = gate_ref[i, :, h * dk:(h + 1) * dk].astype(F32)
        o_ref[i, :, h * dk:(h + 1) * dk] = (on * gate).astype(BF16)
    carry_ref[...] = x_ref[:, tb - SUBLANES:, :]

    @pl.when(b == pl.num_programs(1) - 1)
    def _():
        sout_ref[...] = s_ref[...]


def _gdn(gqkv, tail, gate, prev8, s0, conv_w, gparams, out_norm_w, *, tb, chunk, sq):
    n, t, w = gqkv.shape
    heads, dk = GDN_HEADS, GDN_DK
    assert t % tb == 0 and tb % chunk == 0 and tb % SUBLANES == 0 and n % sq == 0
    tok = lambda width: pl.BlockSpec((sq, tb, width), lambda i, b: (i, b, 0))
    const = lambda a: pl.BlockSpec(a.shape, lambda i, b: (0,) * a.ndim)
    state = pl.BlockSpec((sq, heads, dk, dk), lambda i, b: (i, 0, 0, 0))
    return pl.pallas_call(
        functools.partial(_gdn_kernel, chunk=chunk),
        grid=(n // sq, t // tb),
        in_specs=[tok(w), tok(LANES), tok(heads * dk),
                  pl.BlockSpec((sq, SUBLANES, w), lambda i, b: (i, 0, 0)), state,
                  const(conv_w), const(gparams), const(out_norm_w)],
        out_specs=(tok(heads * dk), state),
        out_shape=(jax.ShapeDtypeStruct((n, t, heads * dk), BF16),
                   jax.ShapeDtypeStruct((n, heads, dk, dk), F32)),
        scratch_shapes=[pltpu.VMEM((sq, heads, dk, dk), F32), pltpu.VMEM((sq, SUBLANES, w), F32)],
        compiler_params=_cparams(("parallel", "arbitrary")), name="gdn",
    )(gqkv, tail, gate, prev8, s0, conv_w, gparams, out_norm_w)


def _tail_kernel(x_ref, ma_ref, mb_ref, prev_ref, wo_ref, nw_ref, win_ref, cw_ref, wout_ref, fnw_ref,
                 y_ref, last_ref, carry_ref, *, tt):
    b = pl.program_id(1)
    rows, d = x_ref.shape
    sa = ma_ref.shape[1]

    @pl.when(b == 0)
    def _():
        carry_ref[...] = prev_ref[...]

    h1 = x_ref[...] + (jnp.dot(ma_ref[...], wo_ref[:sa, :], preferred_element_type=F32)
                       + jnp.dot(mb_ref[...], wo_ref[sa:, :], preferred_element_type=F32))
    xn = _rms(h1, nw_ref[...]).astype(BF16)
    proj = lambda i: jnp.dot(xn, win_ref[:, i * d:(i + 1) * d], preferred_element_type=F32)
    cu = proj(2) * proj(0)
    cw = cw_ref[...]
    if tt == rows:
        yc = _causal_conv_rows(cu, carry_ref[...], cw, SC_CONV)
        carry_ref[...] = cu[rows - SUBLANES:, :]
        last_ref[...] = cu[rows - SUBLANES:, :]
    else:
        assert tt == SUBLANES
        prev = carry_ref[...]
        tok = lax.broadcasted_iota(jnp.int32, (rows, d), 0) % tt
        yc = cu * cw[SC_CONV - 1:SC_CONV, :]
        for s in range(1, SC_CONV):
            shifted = jnp.where(tok >= s, pltpu.roll(cu, s, 0), pltpu.roll(prev, (s - tt) % rows, 0))
            yc = yc + shifted * cw[SC_CONV - 1 - s:SC_CONV - s, :]
        last_ref[...] = cu
    m = (proj(1) * yc * _silu(proj(3))).astype(BF16)
    h2 = h1 + jnp.dot(m, wout_ref[...], preferred_element_type=F32)
    y_ref[...] = _rms(h2, fnw_ref[...])


def _tail(x, mix_a, mix_b, prev, w_out0, norm_w1, sc_w_in, sc_conv_w, sc_w_out, final_norm_w, *, tm, tt):
    g, r, d = x.shape
    assert r % tm == 0 and tm % tt == 0 and (tt == tm or (tt == SUBLANES and r == tm))
    nseq = tm // tt
    tok = lambda width: pl.BlockSpec((None, tm, width), lambda i, b: (i, b, 0))
    const = lambda a: pl.BlockSpec(a.shape, lambda i, b: (0,) * a.ndim, pipeline_mode=pl.Buffered(1))
    small = pl.BlockSpec((None, nseq * SUBLANES, d), lambda i, b: (i, 0, 0))
    return pl.pallas_call(
        functools.partial(_tail_kernel, tt=tt),
        grid=(g, r // tm),
        in_specs=[tok(d), tok(mix_a.shape[2]), tok(mix_b.shape[2]), small,
                  const(w_out0), const(norm_w1), const(sc_w_in), const(sc_conv_w), const(sc_w_out),
                  const(final_norm_w)],
        out_specs=(tok(d), small),
        out_shape=(jax.ShapeDtypeStruct((g, r, d), F32),
                   jax.ShapeDtypeStruct((g, nseq * SUBLANES, d), F32)),
        scratch_shapes=[pltpu.VMEM((nseq * SUBLANES, d), F32)],
        compiler_params=_cparams(("parallel", "arbitrary")), name="tail",
    )(x, mix_a, mix_b, prev, w_out0, norm_w1, sc_w_in, sc_conv_w, sc_w_out, final_norm_w)


def _pad_front_rows(buf, rows):
    return jnp.pad(buf, ((0, 0), (rows - buf.shape[1], 0), (0, 0)))


def _trunk(x, weights, sb_fn, gdn_state, gdn_buf, sc_buf, *, kv_transposed, tm_proj, gdn_tb, gdn_chunk, gdn_sq,
           tail_tm, tail_tt):
    (norm_w, w_main, w_kvt, w_tail, conv_w, gparams, out_norm_w, w_out0, sc_w_in, sc_conv_w, sc_w_out,
     final_norm_w) = weights
    n, t, d = x.shape
    sbw = w_out0.shape[0] // 2
    heads = sbw // SB_HEAD_DIM
    gvw = GDN_HEADS * GDN_DK
    gqkvw = conv_w.shape[1]
    xp = x if kv_transposed else x.reshape(1, n * t, d)
    q, kb, vb, kf, vf, sg, gqkv, gg, tail = _proj0(
        xp, norm_w[0:1], w_main, w_kvt, w_tail, sbw=sbw, gqkvw=gqkvw, gvw=gvw, tm=tm_proj,
        kv_transposed=kv_transposed)
    r3 = lambda a: a.reshape(n, t, a.shape[-1])
    if kv_transposed:
        new_kv = lambda a: jnp.transpose(a.reshape(n, heads, SB_HEAD_DIM, t), (0, 3, 1, 2))[None]
        mix_a = sb_fn(q, kb, vb, sg)
    else:
        new_kv = lambda a: a.reshape(1, n, t, heads, SB_HEAD_DIM)
        mix_a = sb_fn(r3(q), r3(kb), r3(vb), r3(sg))
    new_k, new_v = new_kv(kf), new_kv(vf)
    gqkv3 = r3(gqkv)
    mix_b, new_state = _gdn(gqkv3, r3(tail), r3(gg), _pad_front_rows(gdn_buf, SUBLANES), gdn_state,
                            conv_w, gparams, out_norm_w, tb=gdn_tb, chunk=gdn_chunk, sq=gdn_sq)
    prev_sc = _pad_front_rows(sc_buf, SUBLANES)
    if tail_tt == tail_tm:
        groups = (n, t)
    else:
        groups = (1, n * t)
        prev_sc = prev_sc.reshape(1, n * SUBLANES, d)
    shp = lambda a: a.reshape(groups + (a.shape[-1],))
    y, last = _tail(shp(x), shp(mix_a), shp(mix_b), prev_sc, w_out0, norm_w[1:2], sc_w_in, sc_conv_w,
                    sc_w_out, final_norm_w, tm=tail_tm, tt=tail_tt)
    new_gbuf = gqkv3[:, t - (GDN_CONV - 1):, :][None]
    new_scbuf = last.reshape(n, SUBLANES, d)[:, SUBLANES - (SC_CONV - 1):, :][None]
    return y.reshape(n, t, d), new_k, new_v, new_state[None], new_gbuf, new_scbuf


def kernel(x_prompt, x_sample, cache_sb_k, cache_sb_v, state_gdn, state_gdn_conv, state_sconv, page_table,
           norm_w, ab_w_in, ab_sb_bias, ab_conv_w, ab_a_log, ab_dt_bias, ab_out_norm_w, ab_w_out,
           sc_w_in, sc_conv_w, sc_w_out, final_norm_w):
    assert ab_w_in.shape[0] == 1 and sc_w_in.shape[0] == 1, "one layer of each kind"
    nb, t, d = x_prompt.shape
    ns, ts, _ = x_sample.shape
    heads = GDN_HEADS
    main_cols = ab_w_in.shape[2] - 2 * heads
    sbw = ab_w_out.shape[1] // 2
    w_main = ab_w_in[0, :, :main_cols].astype(BF16)
    w_kvt = ab_w_in[0, :, sbw:3 * sbw].T.astype(BF16)
    w_tail = jnp.pad(ab_w_in[0, :, main_cols:], ((0, 0), (0, LANES - 2 * heads))).astype(BF16)
    gparams = jnp.zeros((SUBLANES, LANES), F32)
    gparams = gparams.at[0, heads:2 * heads].set(ab_a_log[0]).at[1, heads:2 * heads].set(ab_dt_bias[0])
    weights = (norm_w, w_main, w_kvt, w_tail, ab_conv_w[0], gparams, ab_out_norm_w[0:1],
               ab_w_out[0].astype(BF16), sc_w_in[0].astype(BF16), sc_conv_w[0], sc_w_out[0].astype(BF16),
               final_norm_w[None])
    bias = ab_sb_bias[0]

    zeros = lambda *s: jnp.zeros(s, F32)
    tm = min(512, t)
    y_p, k_p, v_p, s_p, gb_p, sc_p = _trunk(
        x_prompt, weights, functools.partial(_sb_prompt, bias, tq=min(256, t), gp=4),
        zeros(nb, heads, GDN_DK, GDN_DK), zeros(nb, GDN_CONV - 1, ab_conv_w.shape[2]),
        zeros(nb, SC_CONV - 1, d),
        kv_transposed=True, tm_proj=tm, gdn_tb=min(256, t), gdn_chunk=min(64, t), gdn_sq=4, tail_tm=tm, tail_tt=tm)

    n_pool, page = cache_sb_k.shape[1], cache_sb_k.shape[2]
    pool_t = lambda c: jnp.transpose(c, (0, 1, 3, 4, 2)).reshape(n_pool, -1, page)
    kpool = pool_t(cache_sb_k)
    vpool = pool_t(cache_sb_v)
    sb_sample = lambda q, kb, vb, sg: _sb_paged(page_table, bias, q, kb, vb, sg, kpool, vpool,
                                                pp=min(32, page_table.shape[1]))
    y_s, k_s, v_s, s_s, gb_s, sc_s = _trunk(
        x_sample, weights, sb_sample, state_gdn[0], state_gdn_conv[0], state_sconv[0],
        kv_transposed=False, tm_proj=ns * ts, gdn_tb=ts, gdn_chunk=ts, gdn_sq=2, tail_tm=ns * ts, tail_tt=ts)
    return (y_p, y_s, k_p, v_p, k_s, v_s, s_p, s_s, gb_p, gb_s, sc_p, sc_s)
```
